```python
import math
import jax, jax.numpy as jnp
from jax import lax
import numpy as np

D_MODEL = 2048
BATCH = 8
SEQ = 2048
DEPTH = 2

F32 = jnp.float32
N_EVEN = (DEPTH + 1) // 2
N_ODD = DEPTH // 2
DN_ALPHA = float((2 * DEPTH) ** 0.25)
DN_BETA = float((8 * DEPTH) ** -0.25)
LN_EPS = 1e-5
RMS_EPS = 1e-6
NEG_INF = -1e30

SSM_HEAD_DIM = 64
SSM_INNER = D_MODEL
SSM_HEADS = SSM_INNER // SSM_HEAD_DIM
SSM_STATE = 128
SSM_GROUPS = 4
SSM_CONV = 4
SSM_CHUNK = 128
SSM_BC = SSM_GROUPS * SSM_STATE
SSM_CONV_DIM = SSM_INNER + 2 * SSM_BC

NSA_HEADS = 16
NSA_GROUPS = 2
NSA_HEAD_DIM = D_MODEL // NSA_HEADS
NSA_WIDTH = NSA_HEADS * NSA_HEAD_DIM
NSA_KV = NSA_GROUPS * NSA_HEAD_DIM
NSA_CMP_BLOCK = 32
NSA_CMP_STRIDE = 16
NSA_SEL_BLOCK = 64
NSA_TOP_N = 8
NSA_WINDOW = 512
NSA_WIN_Q_BLOCK = 128
NSA_SEL_Q_BLOCK = 64
NSA_FORCED = 1e6

AB_SIZES = (SSM_INNER, SSM_CONV_DIM, SSM_HEADS, NSA_WIDTH, 6 * NSA_KV, 3 * NSA_HEADS)
AB_IN = sum(AB_SIZES)
AB_MIX = SSM_INNER + NSA_WIDTH

MLA_HEADS = 16
MLA_Q_RANK = 512
MLA_KV_RANK = 512
MLA_NOPE = 128
MLA_ROPE = 64
MLA_V = 128
ROPE_THETA = 10000.0
ATTN_Q_BLOCK = 128

HGRN_HEADS = 16
HGRN_KEY = 128
HGRN_VAL = 128
HGRN_CHUNK = 64
HGRN_FDIM = HGRN_HEADS * HGRN_KEY
HGRN_WIDTH = HGRN_HEADS * HGRN_VAL

CD_SIZES = (MLA_Q_RANK, MLA_KV_RANK, MLA_ROPE, HGRN_FDIM, HGRN_FDIM, HGRN_WIDTH, HGRN_WIDTH)
CD_IN = sum(CD_SIZES)
CD_MIX = MLA_HEADS * MLA_V + HGRN_WIDTH

N_EXPERTS = 64
TOP_K = 8
N_EXPERT_GROUPS = 8
TOP_GROUPS = 4
EXPERT_FF = 512
SHARED_FF = 512
ROUTED_SCALE = 2.5
MOE_ROW_BLOCK = 256

kernel_name = 'hybrid_ssd_nsa_mla_hgrn2_moe_deepnorm'


def split_cols(h, sizes):
    offs = [int(o) for o in np.cumsum(sizes)[:-1]]
    return jnp.split(h, offs, axis=-1)


def layer_norm(x, g, b):
    xf = x.astype(F32)
    mu = jnp.mean(xf, -1, keepdims=True)
    var = jnp.mean(jnp.square(xf - mu), -1, keepdims=True)
    return ((xf - mu) * lax.rsqrt(var + LN_EPS)).astype(x.dtype) * g + b


def rms_norm(x, w):
    xf = x.astype(F32)
    return (xf * lax.rsqrt(jnp.mean(xf * xf, -1, keepdims=True) + RMS_EPS)).astype(x.dtype) * w


def masked_softmax(s, mask):
    return jax.nn.softmax(jnp.where(mask, s.astype(F32), NEG_INF), axis=-1)


def causal_depthwise_conv(u, w, b):
    width, ch = w.shape
    out = lax.conv_general_dilated(u, w[:, None, :], window_strides=(1,), padding=[(width - 1, 0)],
                                   dimension_numbers=('NWC', 'WIO', 'NWC'), feature_group_count=ch)
    return out + b


def segsum(a):
    t = a.shape[-1]
    cs = jnp.cumsum(a, -1)
    diff = cs[..., :, None] - cs[..., None, :]
    return jnp.where(jnp.tril(jnp.ones((t, t), bool)), diff, -jnp.inf)


def ssd_chunked_scan(xh, dt, a, bm, cm):
    bsz, seq, nh, hp = xh.shape
    ng, ns = bm.shape[2], bm.shape[3]
    hg = nh // ng
    L = SSM_CHUNK
    nc = seq // L
    x = (xh.astype(F32) * dt[..., None]).reshape(bsz, nc, L, ng, hg, hp)
    adt = (dt * a).reshape(bsz, nc, L, ng, hg).transpose(0, 1, 3, 4, 2)
    bmf = bm.astype(F32).reshape(bsz, nc, L, ng, ns)
    cmf = cm.astype(F32).reshape(bsz, nc, L, ng, ns)
    a_cs = jnp.cumsum(adt, -1)
    lmat = jnp.exp(segsum(adt))
    cb = jnp.einsum('bclgn,bcsgn->bcgls', cmf, bmf)
    y_diag = jnp.einsum('bcgls,bcghls,bcsghp->bclghp', cb, lmat, x)
    decay_states = jnp.exp(a_cs[..., -1:] - a_cs)
    states = jnp.einsum('bclgn,bcghl,bclghp->bcghpn', bmf, decay_states, x)
    chunk_a = jnp.moveaxis(a_cs[..., -1], 1, -1)
    decay_chunk = jnp.exp(segsum(jnp.pad(chunk_a, ((0, 0), (0, 0), (0, 0), (1, 0)))))
    states0 = jnp.concatenate([jnp.zeros_like(states[:, :1]), states], axis=1)
    new_states = jnp.einsum('bghzc,bcghpn->bzghpn', decay_chunk, states0)
    prev_states = new_states[:, :-1]
    y_off = jnp.einsum('bclgn,bcghpn,bcghl->bclghp', cmf, prev_states, jnp.exp(a_cs))
    return (y_diag + y_off).reshape(bsz, seq, nh, hp)


def mamba2_mixer(z, xbc, dt_raw, conv_w, conv_b, dt_bias, a_log, d_skip, norm_w):
    bsz, seq, _ = z.shape
    xbc = jax.nn.silu(causal_depthwise_conv(xbc, conv_w, conv_b))
    xs, bm, cm = jnp.split(xbc, [SSM_INNER, SSM_INNER + SSM_BC], axis=-1)
    xh = xs.reshape(bsz, seq, SSM_HEADS, SSM_HEAD_DIM)
    dt = jax.nn.softplus(dt_raw.astype(F32) + dt_bias.astype(F32))
    a = -jnp.exp(a_log.astype(F32))
    y = ssd_chunked_scan(xh, dt, a, bm.reshape(bsz, seq, SSM_GROUPS, SSM_STATE),
                         cm.reshape(bsz, seq, SSM_GROUPS, SSM_STATE))
    y = y + xh.astype(F32) * d_skip.astype(F32)[:, None]
    y = y.reshape(bsz, seq, SSM_INNER).astype(z.dtype)
    return rms_norm(y * jax.nn.silu(z), norm_w)


def nsa_mixer(q, kv, gates, cmp_pos, cmp_w):
    bsz, seq, _ = q.shape
    G, hg, dh = NSA_GROUPS, NSA_HEADS // NSA_GROUPS, NSA_HEAD_DIM
    scale = dh ** -0.5
    q = q.reshape(bsz, seq, G, hg, dh)
    k_cmp, v_cmp, k_sel, v_sel, k_win, v_win = [t.reshape(bsz, seq, G, dh) for t in jnp.split(kv, 6, axis=-1)]
    t_pos = jnp.arange(seq)

    n_cmp = (seq - NSA_CMP_BLOCK) // NSA_CMP_STRIDE + 1
    blk = np.arange(n_cmp)[:, None] * NSA_CMP_STRIDE + np.arange(NSA_CMP_BLOCK)[None, :]

    def compress(t, pos, w):
        blocks = t[:, blk] + pos[:, None, :]
        return jnp.einsum('bilgd,lde->bige', blocks, w.reshape(NSA_CMP_BLOCK, dh, dh))

    kc = compress(k_cmp, cmp_pos[0], cmp_w[0])
    vc = compress(v_cmp, cmp_pos[1], cmp_w[1])
    s_cmp = jnp.einsum('bsghd,bigd->bghsi', q, kc).astype(F32) * scale
    cmask = jnp.asarray(np.arange(n_cmp) * NSA_CMP_STRIDE + NSA_CMP_BLOCK - 1)[None, :] <= t_pos[:, None]
    p_cmp = masked_softmax(s_cmp, cmask) * jnp.any(cmask, -1)[:, None].astype(F32)
    o_cmp = jnp.einsum('bghsi,bigd->bsghd', p_cmp.astype(vc.dtype), vc)

    n_sel = seq // NSA_SEL_BLOCK
    top_n = min(NSA_TOP_N, n_sel)
    ci = np.arange(n_cmp)[:, None]
    sj = np.arange(n_sel)[None, :]
    ov = np.clip(np.minimum(ci * NSA_CMP_STRIDE + NSA_CMP_BLOCK, sj * NSA_SEL_BLOCK + NSA_SEL_BLOCK)
                 - np.maximum(ci * NSA_CMP_STRIDE, sj * NSA_SEL_BLOCK), 0, None) / NSA_CMP_BLOCK
    imp = jnp.einsum('bghsi,ij->bgsj', p_cmp, jnp.asarray(ov, F32))
    jb = jnp.arange(n_sel)[None, :]
    cur = (t_pos // NSA_SEL_BLOCK)[:, None]
    causal_blk = jb * NSA_SEL_BLOCK <= t_pos[:, None]
    forced = (jb == 0) | (jb == cur) | (jb == cur - 1)
    imp = jnp.where(forced, NSA_FORCED, jnp.where(causal_blk, imp, -1.0))
    _, sel_idx = lax.top_k(imp, top_n)
    ksb = k_sel.reshape(bsz, n_sel, NSA_SEL_BLOCK, G, dh).transpose(0, 3, 1, 2, 4)
    vsb = v_sel.reshape(bsz, n_sel, NSA_SEL_BLOCK, G, dh).transpose(0, 3, 1, 2, 4)
    bi = jnp.arange(bsz)[:, None, None, None]
    gi = jnp.arange(G)[None, :, None, None]
    qbs = NSA_SEL_Q_BLOCK

    def sel_block(qb):
        q0 = qb * qbs
        qblk = lax.dynamic_slice_in_dim(q, q0, qbs, axis=1)
        iblk = lax.dynamic_slice_in_dim(sel_idx, q0, qbs, axis=2)
        kg = ksb[bi, gi, iblk]
        vg = vsb[bi, gi, iblk]
        tq = q0 + jnp.arange(qbs)
        kpos = iblk[..., None] * NSA_SEL_BLOCK + jnp.arange(NSA_SEL_BLOCK)
        m = kpos <= tq[None, None, :, None, None]
        s = jnp.einsum('bqghd,bgqnld->bghqnl', qblk, kg).astype(F32) * scale
        s = jnp.where(m[:, :, None], s, NEG_INF)
        p = jax.nn.softmax(s.reshape(s.shape[:4] + (-1,)), axis=-1).reshape(s.shape)
        return jnp.einsum('bghqnl,bgqnld->bqghd', p.astype(vg.dtype), vg)

    o_sel = lax.map(sel_block, jnp.arange(seq // qbs))
    o_sel = jnp.moveaxis(o_sel, 0, 1).reshape(bsz, seq, G, hg, dh)

    W, qbw = NSA_WINDOW, NSA_WIN_Q_BLOCK
    kwp = jnp.pad(k_win, ((0, 0), (W, 0), (0, 0), (0, 0)))
    vwp = jnp.pad(v_win, ((0, 0), (W, 0), (0, 0), (0, 0)))

    def win_block(qb):
        q0 = qb * qbw
        qblk = lax.dynamic_slice_in_dim(q, q0, qbw, axis=1)
        kblk = lax.dynamic_slice_in_dim(kwp, q0, W + qbw, axis=1)
        vblk = lax.dynamic_slice_in_dim(vwp, q0, W + qbw, axis=1)
        tq = (q0 + jnp.arange(qbw))[:, None]
        tk = (q0 - W + jnp.arange(W + qbw))[None, :]
        m = (tk <= tq) & (tk > tq - W) & (tk >= 0)
        s = jnp.einsum('bqghd,bkgd->bghqk', qblk, kblk).astype(F32) * scale
        p = masked_softmax(s, m)
        return jnp.einsum('bghqk,bkgd->bqghd', p.astype(vblk.dtype), vblk)

    o_win = lax.map(win_block, jnp.arange(seq // qbw))
    o_win = jnp.moveaxis(o_win, 0, 1).reshape(bsz, seq, G, hg, dh)

    g = jax.nn.sigmoid(gates.astype(F32)).astype(q.dtype).reshape(bsz, seq, G, hg, 3)
    o = g[..., 0:1] * o_cmp + g[..., 1:2] * o_sel + g[..., 2:3] * o_win
    return o.reshape(bsz, seq, NSA_WIDTH)


def rope_angles(positions):
    inv = 1.0 / (ROPE_THETA ** (jnp.arange(0, MLA_ROPE, 2, dtype=F32) / MLA_ROPE))
    ang = positions.astype(F32)[..., None] * inv
    return jnp.cos(ang), jnp.sin(ang)


def apply_rope(t, cos, sin):
    t1, t2 = jnp.split(t.astype(F32), 2, axis=-1)
    return jnp.concatenate([t1 * cos - t2 * sin, t1 * sin + t2 * cos], -1).astype(t.dtype)


def mla_mixer(c_q, c_kv, k_rope, positions, q_norm_w, w_uq, kv_norm_w, w_ukv):
    bsz, seq, _ = c_q.shape
    H = MLA_HEADS
    q = (rms_norm(c_q, q_norm_w) @ w_uq).reshape(bsz, seq, H, MLA_NOPE + MLA_ROPE)
    q_nope, q_rope = jnp.split(q, [MLA_NOPE], axis=-1)
    kv = (rms_norm(c_kv, kv_norm_w) @ w_ukv).reshape(bsz, seq, H, MLA_NOPE + MLA_V)
    k_nope, v = jnp.split(kv, [MLA_NOPE], axis=-1)
    cos, sin = rope_angles(positions)
    q_rope = apply_rope(q_rope, cos[:, :, None], sin[:, :, None])
    k_rope = apply_rope(k_rope, cos, sin)
    scale = (MLA_NOPE + MLA_ROPE) ** -0.5
    qb_len = ATTN_Q_BLOCK
    kidx = jnp.arange(seq)[None, :]

    def attn_block(qb):
        q0 = qb * qb_len
        qn = lax.dynamic_slice_in_dim(q_nope, q0, qb_len, axis=1)
        qr = lax.dynamic_slice_in_dim(q_rope, q0, qb_len, axis=1)
        s = (jnp.einsum('bqhd,bkhd->bhqk', qn, k_nope)
             + jnp.einsum('bqhd,bkd->bhqk', qr, k_rope)).astype(F32) * scale
        m = kidx <= (q0 + jnp.arange(qb_len))[:, None]
        p = masked_softmax(s, m)
        return jnp.einsum('bhqk,bkhd->bqhd', p.astype(v.dtype), v)

    o = lax.map(attn_block, jnp.arange(seq // qb_len))
    return jnp.moveaxis(o, 0, 1).reshape(bsz, seq, H * MLA_V)


def chunk_gated_linear(q, k, v, log_f):
    bsz, seq, nh, dk = q.shape
    dv = v.shape[-1]
    C = HGRN_CHUNK
    nc = seq // C
    q, k, log_f = [t.reshape(bsz, nc, C, nh, dk) for t in (q, k, log_f)]
    v = v.reshape(bsz, nc, C, nh, dv)
    b = jnp.cumsum(log_f, axis=2)
    b_last = b[:, :, -1:]
    q_t = q * jnp.exp(b)
    k_t = k * jnp.exp(-b)
    k_end = k * jnp.exp(b_last - b)
    att = jnp.einsum('bnthd,bnshd->bnhts', q_t, k_t)
    att = jnp.where(jnp.tril(jnp.ones((C, C), bool)), att, 0.0)
    o_intra = jnp.einsum('bnhts,bnshv->bnthv', att, v)
    d_state = jnp.einsum('bnshd,bnshv->bnhdv', k_end, v)
    decay = jnp.exp(b_last[:, :, 0])

    def step(s_prev, inp):
        d_n, ds_n = inp
        return d_n[..., None] * s_prev + ds_n, s_prev

    _, s_start = lax.scan(step, jnp.zeros((bsz, nh, dk, dv), F32),
                          (jnp.moveaxis(decay, 1, 0), jnp.moveaxis(d_state, 1, 0)))
    s_start = jnp.moveaxis(s_start, 0, 1)
    o_inter = jnp.einsum('bnthd,bnhdv->bnthv', q_t, s_start)
    return (o_intra + o_inter).reshape(bsz, seq, nh, dv)


def hgrn2_mixer(h_q, h_f, h_i, h_g, lb, norm_w):
    bsz, seq, _ = h_q.shape
    q = jax.nn.silu(h_q.astype(F32)).reshape(bsz, seq, HGRN_HEADS, HGRN_KEY)
    hf = h_f.astype(F32)
    f = lb + (1.0 - lb) * jax.nn.sigmoid(hf)
    k = ((1.0 - lb) * jax.nn.sigmoid(-hf)).reshape(bsz, seq, HGRN_HEADS, HGRN_KEY)
    log_f = jnp.log(f).reshape(bsz, seq, HGRN_HEADS, HGRN_KEY)
    v = h_i.astype(F32).reshape(bsz, seq, HGRN_HEADS, HGRN_VAL)
    o = chunk_gated_linear(q, k, v, log_f)
    o = rms_norm(o, norm_w.astype(F32)).reshape(bsz, seq, HGRN_WIDTH).astype(h_g.dtype)
    return o * jax.nn.silu(h_g)


def mixer_ab(x, w_in, conv_w, conv_b, dt_bias, a_log, d_skip, norm_w, cmp_pos, cmp_w, w_out):
    z, xbc, dt_raw, q, kv, gates = split_cols(x @ w_in, AB_SIZES)
    y_a = mamba2_mixer(z, xbc, dt_raw, conv_w, conv_b, dt_bias, a_log, d_skip, norm_w)
    y_b = nsa_mixer(q, kv, gates, cmp_pos, cmp_w)
    return jnp.concatenate([y_a, y_b], axis=-1) @ w_out


def mixer_cd(x, positions, w_in, q_norm_w, w_uq, kv_norm_w, w_ukv, lb, hgrn_norm_w, w_out):
    c_q, c_kv, k_rope, h_q, h_f, h_i, h_g = split_cols(x @ w_in, CD_SIZES)
    y_c = mla_mixer(c_q, c_kv, k_rope, positions, q_norm_w, w_uq, kv_norm_w, w_ukv)
    y_d = hgrn2_mixer(h_q, h_f, h_i, h_g, lb, hgrn_norm_w)
    return jnp.concatenate([y_c, y_d], axis=-1) @ w_out


def moe_ffn(x, w_router, router_bias, w_gate, w_up, w_down, ws_gate, ws_up, ws_down):
    bsz, seq, d = x.shape
    T = bsz * seq
    E = N_EXPERTS
    xf = x.reshape(T, d)
    scores = jax.nn.sigmoid((xf @ w_router).astype(F32))
    biased = scores + router_bias.astype(F32)
    grp_score = lax.top_k(biased.reshape(T, N_EXPERT_GROUPS, E // N_EXPERT_GROUPS), 2)[0].sum(-1)
    _, top_g = lax.top_k(grp_score, TOP_GROUPS)
    gmask = jnp.any(top_g[..., None] == jnp.arange(N_EXPERT_GROUPS), axis=1)
    masked = jnp.where(jnp.repeat(gmask, E // N_EXPERT_GROUPS, axis=1), biased, -jnp.inf)
    _, top_e = lax.top_k(masked, TOP_K)
    w = jnp.take_along_axis(scores, top_e, axis=1)
    w = w / jnp.sum(w, -1, keepdims=True) * ROUTED_SCALE
    A = T * TOP_K
    M = MOE_ROW_BLOCK
    NB = -(-A // M) + E
    P = NB * M
    flat_e = top_e.reshape(A)
    flat_tok = jnp.arange(A, dtype=jnp.int32) // TOP_K
    flat_w = w.reshape(A)
    order = jnp.argsort(flat_e)
    se = flat_e[order]
    counts = jnp.zeros((E,), jnp.int32).at[flat_e].add(1)
    starts = jnp.cumsum(counts) - counts
    pcounts = (counts + M - 1) // M * M
    pends = jnp.cumsum(pcounts)
    pstarts = pends - pcounts
    dest = pstarts[se] + (jnp.arange(A, dtype=jnp.int32) - starts[se])
    row_tok = jnp.full((P,), T, jnp.int32).at[dest].set(flat_tok[order])
    row_w = jnp.zeros((P,), F32).at[dest].set(flat_w[order])
    blk_e = jnp.minimum(jnp.searchsorted(pends, jnp.arange(NB, dtype=jnp.int32) * M, side='right'), E - 1)
    x_pad = jnp.concatenate([xf, jnp.zeros((1, d), xf.dtype)], axis=0)

    def body(i, acc):
        rows = lax.dynamic_slice_in_dim(row_tok, i * M, M)
        rw = lax.dynamic_slice_in_dim(row_w, i * M, M)
        e = blk_e[i]
        xb = x_pad[rows]
        h = jax.nn.silu(xb @ w_gate[e]) * (xb @ w_up[e])
        return acc.at[rows].add((h @ w_down[e]) * rw[:, None].astype(h.dtype))

    routed = lax.fori_loop(0, NB, body, jnp.zeros((T + 1, d), x.dtype))[:T]
    shared = (jax.nn.silu(xf @ ws_gate) * (xf @ ws_up)) @ ws_down
    return (shared + routed).reshape(bsz, seq, d)


def setup_inputs(seed: int = 0) -> dict:
    key = jax.random.key(seed)
    ks = iter(jax.random.split(key, 40))

    def nrm(shape, scale):
        return jax.random.normal(next(ks), shape, F32) * scale

    def gain(shape):
        return 1.0 + 0.02 * jax.random.normal(next(ks), shape, F32)

    x = jax.random.normal(next(ks), (BATCH, SEQ, D_MODEL), F32)
    offs = jax.random.randint(next(ks), (BATCH, 1), 0, 1024, jnp.int32)
    positions = offs + jnp.arange(SEQ, dtype=jnp.int32)[None, :]
    u = jax.random.uniform(next(ks), (N_EVEN, SSM_HEADS), F32)
    dt0 = jnp.exp(u * (math.log(0.1) - math.log(1e-3)) + math.log(1e-3))
    ssm_dt_bias = dt0 + jnp.log(-jnp.expm1(-dt0))
    ssm_a_log = jnp.log(jax.random.uniform(next(ks), (N_EVEN, SSM_HEADS), F32, 1.0, 16.0))
    return {
        'x': x,
        'positions': positions,
        'ab_w_in': nrm((N_EVEN, D_MODEL, AB_IN), D_MODEL ** -0.5),
        'ssm_conv_w': nrm((N_EVEN, SSM_CONV, SSM_CONV_DIM), SSM_CONV ** -0.5),
        'ssm_conv_b': nrm((N_EVEN, SSM_CONV_DIM), 0.01),
        'ssm_dt_bias': ssm_dt_bias,
        'ssm_a_log': ssm_a_log,
        'ssm_d': gain((N_EVEN, SSM_HEADS)),
        'ssm_norm_w': gain((N_EVEN, SSM_INNER)),
        'nsa_cmp_pos': nrm((N_EVEN, 2, NSA_CMP_BLOCK, NSA_HEAD_DIM), 0.02),
        'nsa_cmp_w': nrm((N_EVEN, 2, NSA_CMP_BLOCK * NSA_HEAD_DIM, NSA_HEAD_DIM), (NSA_CMP_BLOCK * NSA_HEAD_DIM) ** -0.5),
        'ab_w_out': nrm((N_EVEN, AB_MIX, D_MODEL), DN_BETA * AB_MIX ** -0.5),
        'cd_w_in': nrm((N_ODD, D_MODEL, CD_IN), D_MODEL ** -0.5),
        'mla_q_norm_w': gain((N_ODD, MLA_Q_RANK)),
        'mla_w_uq': nrm((N_ODD, MLA_Q_RANK, MLA_HEADS * (MLA_NOPE + MLA_ROPE)), MLA_Q_RANK ** -0.5),
        'mla_kv_norm_w': gain((N_ODD, MLA_KV_RANK)),
        'mla_w_ukv': nrm((N_ODD, MLA_KV_RANK, MLA_HEADS * (MLA_NOPE + MLA_V)), MLA_KV_RANK ** -0.5),
        'hgrn_lb_logits': nrm((DEPTH, HGRN_FDIM), 0.1),
        'hgrn_norm_w': gain((N_ODD, HGRN_VAL)),
        'cd_w_out': nrm((N_ODD, CD_MIX, D_MODEL), DN_BETA * CD_MIX ** -0.5),
        'ln_g': gain((DEPTH, 2, D_MODEL)),
        'ln_b': nrm((DEPTH, 2, D_MODEL), 0.01),
        'moe_w_router': nrm((DEPTH, D_MODEL, N_EXPERTS), D_MODEL ** -0.5),
        'moe_router_bias': nrm((DEPTH, N_EXPERTS), 0.01),
        'moe_w_gate': nrm((DEPTH, N_EXPERTS, D_MODEL, EXPERT_FF), D_MODEL ** -0.5),
        'moe_w_up': nrm((DEPTH, N_EXPERTS, D_MODEL, EXPERT_FF), D_MODEL ** -0.5),
        'moe_w_down': nrm((DEPTH, N_EXPERTS, EXPERT_FF, D_MODEL), DN_BETA * EXPERT_FF ** -0.5),
        'moe_shared_w_gate': nrm((DEPTH, D_MODEL, SHARED_FF), D_MODEL ** -0.5),
        'moe_shared_w_up': nrm((DEPTH, D_MODEL, SHARED_FF), D_MODEL ** -0.5),
        'moe_shared_w_down': nrm((DEPTH, SHARED_FF, D_MODEL), DN_BETA * SHARED_FF ** -0.5),
    }


def reference(x, positions, ab_w_in, ssm_conv_w, ssm_conv_b, ssm_dt_bias, ssm_a_log, ssm_d, ssm_norm_w,
              nsa_cmp_pos, nsa_cmp_w, ab_w_out, cd_w_in, mla_q_norm_w, mla_w_uq, mla_kv_norm_w, mla_w_ukv,
              hgrn_lb_logits, hgrn_norm_w, cd_w_out, ln_g, ln_b, moe_w_router, moe_router_bias,
              moe_w_gate, moe_w_up, moe_w_down, moe_shared_w_gate, moe_shared_w_up, moe_shared_w_down):
    lb_all = jnp.cumsum(jax.nn.softmax(hgrn_lb_logits.astype(F32), axis=0), axis=0)
    lb_all = lb_all - lb_all[0]
    for l in range(DEPTH):
        i = l // 2
        if l % 2 == 0:
            mix = mixer_ab(x, ab_w_in[i], ssm_conv_w[i], ssm_conv_b[i], ssm_dt_bias[i], ssm_a_log[i],
                           ssm_d[i], ssm_norm_w[i], nsa_cmp_pos[i], nsa_cmp_w[i], ab_w_out[i])
        else:
            mix = mixer_cd(x, positions, cd_w_in[i], mla_q_norm_w[i], mla_w_uq[i], mla_kv_norm_w[i],
                           mla_w_ukv[i], lb_all[l], hgrn_norm_w[i], cd_w_out[i])
        x = layer_norm(DN_ALPHA * x + mix, ln_g[l, 0], ln_b[l, 0])
        ffn = moe_ffn(x, moe_w_router[l], moe_router_bias[l], moe_w_gate[l], moe_w_up[l], moe_w_down[l],
                      moe_shared_w_gate[l], moe_shared_w_up[l], moe_shared_w_down[l])
        x = layer_norm(DN_ALPHA * x + ffn, ln_g[l, 1], ln_b[l, 1])
    return x
```

```python
import functools

import numpy as np
import jax
import jax.numpy as jnp
from jax import lax
from jax.experimental import pallas as pl
from jax.experimental.pallas import tpu as pltpu

F32 = jnp.float32
BF16 = jnp.bfloat16

D_MODEL = 2048
DEPTH = 2
DN_ALPHA = float((2 * DEPTH) ** 0.25)
LN_EPS = 1e-5
RMS_EPS = 1e-6
NEG_INF = -1e30

SSM_HEAD_DIM = 64
SSM_INNER = D_MODEL
SSM_HEADS = SSM_INNER // SSM_HEAD_DIM
SSM_STATE = 128
SSM_GROUPS = 4
SSM_CONV = 4
SSM_CHUNK = 128
SSM_BC = SSM_GROUPS * SSM_STATE
SSM_CONV_DIM = SSM_INNER + 2 * SSM_BC
SSM_GROUP_W = SSM_INNER // SSM_GROUPS

NSA_HEADS = 16
NSA_GROUPS = 2
NSA_HG = NSA_HEADS // NSA_GROUPS
NSA_HEAD_DIM = D_MODEL // NSA_HEADS
NSA_WIDTH = NSA_HEADS * NSA_HEAD_DIM
NSA_KV = NSA_GROUPS * NSA_HEAD_DIM
NSA_CMP_BLOCK = 32
NSA_CMP_STRIDE = 16
NSA_SEL_BLOCK = 64
NSA_TOP_N = 8
NSA_WINDOW = 512
NSA_FORCED = 1e6
NSA_Q_TILE = 128
NSA_SEL_K_TILE = 512

MLA_HEADS = 16
MLA_Q_RANK = 512
MLA_KV_RANK = 512
MLA_NOPE = 128
MLA_ROPE = 64
MLA_V = 128
ROPE_THETA = 10000.0
MLA_TILE = 512

HGRN_HEADS = 16
HGRN_KEY = 128
HGRN_VAL = 128
HGRN_CHUNK = 64
HGRN_FDIM = HGRN_HEADS * HGRN_KEY
HGRN_WIDTH = HGRN_HEADS * HGRN_VAL
HGRN_TILE = 512

N_EXPERTS = 64
TOP_K = 8
N_EXPERT_GROUPS = 8
TOP_GROUPS = 4
EXPERT_FF = 512
ROUTED_SCALE = 2.5
MOE_ROW_BLOCK = 256

LANES = 128
VMEM_LIMIT = 48 * 1024 * 1024


def _cp(*sem):
    return pltpu.CompilerParams(dimension_semantics=sem, vmem_limit_bytes=VMEM_LIMIT)


def _dot(a, b):
    return jnp.dot(a.astype(BF16), b.astype(BF16), preferred_element_type=F32)


def _dot_nt(a, b):
    return lax.dot_general(a.astype(BF16), b.astype(BF16), (((1,), (1,)), ((), ())),
                           preferred_element_type=F32)


def _split3(a):
    a1 = a.astype(BF16)
    r1 = a - a1.astype(F32)
    a2 = r1.astype(BF16)
    r2 = r1 - a2.astype(F32)
    return a1, a2, r2.astype(BF16)


def _dot_f32_by_exact(a, t):
    t = t.astype(BF16)
    return sum(jnp.dot(p, t, preferred_element_type=F32) for p in _split3(a))


def _dot_exact_by_f32(t, a):
    t = t.astype(BF16)
    return sum(jnp.dot(t, p, preferred_element_type=F32) for p in _split3(a))


def _silu(x):
    return x * jax.nn.sigmoid(x)


def _softplus(x):
    return jnp.maximum(x, 0.0) + jnp.log(1.0 + jnp.exp(-jnp.abs(x)))


def _iota(shape, dim):
    return lax.broadcasted_iota(jnp.int32, shape, dim)


def _mm_kernel(x_ref, w_ref, o_ref):
    o_ref[...] = jnp.dot(x_ref[...], w_ref[...], preferred_element_type=F32).astype(o_ref.dtype)


def _matmul(x, w, out_dtype, tm=512, tn=512):
    m, k = x.shape
    n = w.shape[1]
    tm = min(tm, m)
    tn = min(tn, n)
    assert m % tm == 0 and n % tn == 0, (m, n, tm, tn)
    return pl.pallas_call(
        _mm_kernel,
        grid=(m // tm, n // tn),
        in_specs=[pl.BlockSpec((tm, k), lambda i, j: (i, 0)), pl.BlockSpec((k, tn), lambda i, j: (0, j))],
        out_specs=pl.BlockSpec((tm, tn), lambda i, j: (i, j)),
        out_shape=jax.ShapeDtypeStruct((m, n), out_dtype),
        compiler_params=_cp("parallel", "arbitrary"),
        name="matmul",
    )(x, w)


def _pad_cols(w, mult):
    n = w.shape[1]
    pad = (-n) % mult
    return jnp.pad(w, ((0, 0), (0, pad))) if pad else w


def _ln_kernel(*refs, n_add):
    x_ref = refs[0]
    add_refs = refs[1:1 + n_add]
    g_ref, b_ref, o_ref, obf_ref = refs[1 + n_add:]
    v = DN_ALPHA * x_ref[...]
    for r in add_refs:
        v = v + r[...].astype(F32)
    mu = jnp.mean(v, -1, keepdims=True)
    d = v - mu
    var = jnp.mean(d * d, -1, keepdims=True)
    y = d * lax.rsqrt(var + LN_EPS) * g_ref[...] + b_ref[...]
    o_ref[...] = y
    obf_ref[...] = y.astype(BF16)


def _deepnorm_ln(x, adds, g, b, tm=256):
    m, d = x.shape
    row = pl.BlockSpec((tm, d), lambda i: (i, 0))
    vec = pl.BlockSpec((1, d), lambda i: (0, 0))
    return pl.pallas_call(
        functools.partial(_ln_kernel, n_add=len(adds)),
        grid=(m // tm,),
        in_specs=[row] * (1 + len(adds)) + [vec, vec],
        out_specs=[row, row],
        out_shape=[jax.ShapeDtypeStruct((m, d), F32), jax.ShapeDtypeStruct((m, d), BF16)],
        compiler_params=_cp("parallel"),
        name="deepnorm_ln",
    )(x, *adds, g.reshape(1, d), b.reshape(1, d))


def _ssd_kernel(z_ref, xs_ref, bc_ref, dt_ref, dtt_ref, cwx_ref, cbx_ref, cwb_ref, cbb_ref, dtb_ref, dtbt_ref,
                alog_ref, alogt_ref, dexp_ref, nw_ref, e_ref, o_ref, extx, extb, st, y_acc):
    L = SSM_CHUNK
    halo = 8

    @pl.when(pl.program_id(1) == 0)
    def _():
        extx[0:halo, :] = jnp.zeros((halo, SSM_INNER), F32)
        extb[0:halo, :] = jnp.zeros((halo, 2 * SSM_BC), F32)
        st[...] = jnp.zeros_like(st)

    extx[halo:halo + L, :] = xs_ref[...]
    extb[halo:halo + L, :] = bc_ref[...]

    def conv(ext, w_ref, b_ref):
        acc = b_ref[...]
        for k in range(SSM_CONV):
            acc = acc + ext[pl.ds(halo - (SSM_CONV - 1) + k, L), :] * w_ref[k:k + 1, :]
        return acc

    xs = _silu(conv(extx, cwx_ref, cbx_ref))
    bc = _silu(conv(extb, cwb_ref, cbb_ref))
    extx[0:halo, :] = xs_ref[L - halo:L, :]
    extb[0:halo, :] = bc_ref[L - halo:L, :]

    li = _iota((L, L), 0)
    si = _iota((L, L), 1)
    low = si <= li
    dt = _softplus(dt_ref[:, 0:SSM_HEADS] + dtb_ref[...])
    adt = dt * (-jnp.exp(alog_ref[...]))
    a_cs = _dot_exact_by_f32(jnp.where(low, 1.0, 0.0), adt)
    dtt = _softplus(dtt_ref[...] + dtbt_ref[...])
    adtt = dtt * (-jnp.exp(alogt_ref[...]))
    a_cst = _dot_f32_by_exact(adtt, jnp.where(li <= si, 1.0, 0.0))

    e = e_ref[...]
    dt_e = _dot_f32_by_exact(dt, e)
    acs_e = _dot_f32_by_exact(a_cs, e)
    a_last = acs_e[L - 1:L, :]
    x_dt = xs * dt_e
    xw = (x_dt * jnp.exp(a_last - acs_e)).astype(BF16)
    ea = jnp.exp(acs_e)
    chunk_decay = jnp.exp(a_last)
    x_dt_b = x_dt.astype(BF16)
    lane = _iota((L, 2 * SSM_HEAD_DIM), 1)

    for g in range(SSM_GROUPS):
        gc = slice(g * SSM_GROUP_W, (g + 1) * SSM_GROUP_W)
        bm = bc[:, g * SSM_STATE:(g + 1) * SSM_STATE]
        cm = bc[:, SSM_BC + g * SSM_STATE:SSM_BC + (g + 1) * SSM_STATE]
        cb = _dot_nt(cm, bm)
        st_g = st[:, gc]
        y_acc[:, gc] = _dot(cm, st_g) * ea[:, gc]
        for pr in range(SSM_GROUP_W // (2 * SSM_HEAD_DIM)):
            h0 = (g * SSM_GROUP_W) // SSM_HEAD_DIM + 2 * pr
            cols = slice(h0 * SSM_HEAD_DIM, (h0 + 2) * SSM_HEAD_DIM)
            xr = x_dt_b[:, cols]
            res = []
            for h in (h0, h0 + 1):
                seg = a_cs[:, h:h + 1] - a_cst[h:h + 1, :]
                res.append(_dot(cb * jnp.where(low, jnp.exp(seg), 0.0), xr))
            y_acc[:, cols] += jnp.where(lane < SSM_HEAD_DIM, res[0], res[1])
        st[:, gc] = st_g * chunk_decay[:, gc] + _dot(bm.T, xw[:, gc])

    y = y_acc[...] + xs * dexp_ref[...]
    y = y * _silu(z_ref[...])
    var = jnp.mean(y * y, -1, keepdims=True)
    o_ref[...] = (y * lax.rsqrt(var + RMS_EPS) * nw_ref[...]).astype(o_ref.dtype)


def _ssd_mixer(zx, small, dt_t, conv_w, conv_b, dt_bias, a_log, d_skip, norm_w):
    bsz, seq, _ = zx.shape
    L = SSM_CHUNK
    nblk = SSM_INNER // (2 * SSM_BC)
    e = (np.arange(SSM_INNER)[None, :] // SSM_HEAD_DIM == np.arange(SSM_HEADS)[:, None])
    full = lambda shape: pl.BlockSpec(shape, lambda b, c: (0,) * len(shape))
    return pl.pallas_call(
        _ssd_kernel,
        grid=(bsz, seq // L),
        in_specs=[
            pl.BlockSpec((None, L, SSM_INNER), lambda b, c: (b, c, 0)),
            pl.BlockSpec((None, L, SSM_INNER), lambda b, c: (b, c, 1)),
            pl.BlockSpec((None, L, 2 * SSM_BC), lambda b, c: (b, c, 2 * nblk)),
            pl.BlockSpec((None, L, LANES), lambda b, c: (b, c, 0)),
            pl.BlockSpec((None, SSM_HEADS, L), lambda b, c: (b, 0, c)),
            full((SSM_CONV, SSM_INNER)), full((1, SSM_INNER)),
            full((SSM_CONV, 2 * SSM_BC)), full((1, 2 * SSM_BC)),
            full((1, SSM_HEADS)), full((SSM_HEADS, 1)), full((1, SSM_HEADS)), full((SSM_HEADS, 1)),
            full((1, SSM_INNER)), full((1, SSM_INNER)), full((SSM_HEADS, SSM_INNER)),
        ],
        out_specs=pl.BlockSpec((None, L, SSM_INNER), lambda b, c: (b, c, 0)),
        out_shape=jax.ShapeDtypeStruct((bsz, seq, SSM_INNER), BF16),
        scratch_shapes=[
            pltpu.VMEM((L + 8, SSM_INNER), F32), pltpu.VMEM((L + 8, 2 * SSM_BC), F32),
            pltpu.VMEM((SSM_STATE, SSM_INNER), F32), pltpu.VMEM((L, SSM_INNER), F32),
        ],
        compiler_params=_cp("arbitrary", "arbitrary"),
        name="ssd_mixer",
    )(zx, zx, zx, small, dt_t,
      conv_w[:, :SSM_INNER], conv_b[None, :SSM_INNER], conv_w[:, SSM_INNER:], conv_b[None, SSM_INNER:],
      dt_bias[None, :], dt_bias[:, None], a_log[None, :], a_log[:, None],
      jnp.repeat(d_skip, SSM_HEAD_DIM)[None, :], norm_w[None, :], jnp.asarray(e, BF16))


def _nsa_compress_kernel(r_ref, w_ref, pos_ref, o_ref):
    ncp = r_ref.shape[0]
    half = NSA_CMP_STRIDE * NSA_HEAD_DIM
    r = r_ref[...].astype(F32)
    top = _dot(r + pos_ref[0:1, :], w_ref[0:half, :])
    bot = _dot(r + pos_ref[1:2, :], w_ref[half:2 * half, :])
    kc = top + pltpu.roll(bot, ncp - 1, axis=0)
    o_ref[...] = jnp.where(_iota(kc.shape, 0) < ncp - 1, kc, 0.0).astype(o_ref.dtype)


def _nsa_compress(strips, cmp_w, cmp_pos):
    bsz, _, ncp, width = strips.shape
    return pl.pallas_call(
        _nsa_compress_kernel,
        grid=(bsz, 2 * NSA_GROUPS),
        in_specs=[
            pl.BlockSpec((None, None, ncp, width), lambda b, j: (b, j, 0, 0)),
            pl.BlockSpec((None, 2 * width, NSA_HEAD_DIM), lambda b, j: (j // NSA_GROUPS, 0, 0)),
            pl.BlockSpec((None, 2, width), lambda b, j: (j // NSA_GROUPS, 0, 0)),
        ],
        out_specs=pl.BlockSpec((None, None, ncp, NSA_HEAD_DIM), lambda b, j: (b, j, 0, 0)),
        out_shape=jax.ShapeDtypeStruct((bsz, 2 * NSA_GROUPS, ncp, NSA_HEAD_DIM), BF16),
        compiler_params=_cp("parallel", "arbitrary"),
        name="nsa_compress",
    )(strips, cmp_w.astype(BF16), cmp_pos.reshape(2, 2, width))


def _softmax_rows(s):
    m = jnp.max(s, -1, keepdims=True)
    p = jnp.exp(s - m)
    return p / jnp.sum(p, -1, keepdims=True)


def _nsa_kernel(q_ref, kvc_ref, ksel_ref, vsel_ref, kwin_ref, vwin_ref, gate_ref, ov_ref, o_ref,
                m_ref, l_ref, acc_ref, *, seq):
    tq = NSA_Q_TILE
    tk = NSA_SEL_K_TILE
    dh = NSA_HEAD_DIM
    rows = NSA_HG * tq
    ncp = seq // NSA_CMP_STRIDE
    n_cmp = ncp - 1
    n_sel = seq // NSA_SEL_BLOCK
    top_n = min(NSA_TOP_N, n_sel)
    wlen = NSA_WINDOW + tq
    scale = dh ** -0.5
    q0 = pl.program_id(1) * tq

    t_row = q0 + (_iota((rows, 1), 0) & (tq - 1))
    gates = jax.nn.sigmoid(gate_ref[...])
    tile8 = lambda a: jnp.concatenate([a] * NSA_HG, axis=0)

    for g in range(NSA_GROUPS):
        gc = slice(g * dh, (g + 1) * dh)
        qs = jnp.concatenate([q_ref[:, (g * NSA_HG + h) * dh:(g * NSA_HG + h + 1) * dh] for h in range(NSA_HG)],
                             axis=0)

        s = _dot_nt(qs, kvc_ref[g]) * scale
        ci = _iota((rows, ncp), 1)
        valid = (ci * NSA_CMP_STRIDE + (NSA_CMP_BLOCK - 1) <= t_row) & (ci < n_cmp)
        p = _softmax_rows(jnp.where(valid, s, NEG_INF))
        p = jnp.where(t_row >= NSA_CMP_BLOCK - 1, p, 0.0)
        o_cmp = _dot(p, kvc_ref[NSA_GROUPS + g])
        psum = p[0:tq]
        for h in range(1, NSA_HG):
            psum = psum + p[h * tq:(h + 1) * tq]

        imp = _dot_f32_by_exact(psum, ov_ref[...])
        tok = q0 + _iota((tq, LANES), 0)
        j = _iota((tq, LANES), 1)
        cur = tok >> 6
        forced = (j == 0) | (j == cur) | (j == cur - 1)
        imp = jnp.where(forced, NSA_FORCED, jnp.where(j * NSA_SEL_BLOCK <= tok, imp, -1.0))
        imp = jnp.where(j < n_sel, imp, -2.0)
        jf = j.astype(F32)
        sel = jnp.zeros((tq, LANES), F32)
        for _ in range(top_n):
            mx = jnp.max(imp, -1, keepdims=True)
            first = jnp.min(jnp.where(imp == mx, jf, 1e9), -1, keepdims=True)
            hit = jf == first
            sel = jnp.where(hit, 1.0, sel)
            imp = jnp.where(hit, -3.0, imp)
        selb = sel.astype(BF16)

        m_ref[...] = jnp.full((rows, 1), NEG_INF, F32)
        l_ref[...] = jnp.zeros((rows, 1), F32)
        acc_ref[...] = jnp.zeros((rows, dh), F32)

        def sel_step(kt, carry):
            k0 = pl.multiple_of(kt * tk, tk)
            kb = ksel_ref[pl.ds(k0, tk), gc]
            vb = vsel_ref[pl.ds(k0, tk), gc]
            sc = _dot_nt(qs, kb) * scale
            blk_of_key = (k0 + _iota((LANES, tk), 1)) >> 6
            expand = jnp.where(blk_of_key == _iota((LANES, tk), 0), 1.0, 0.0).astype(BF16)
            picked = jnp.dot(selb, expand, preferred_element_type=F32)
            ok = (picked > 0.5) & (k0 + _iota((tq, tk), 1) <= q0 + _iota((tq, tk), 0))
            sc = sc + tile8(jnp.where(ok, 0.0, NEG_INF))
            m_old = m_ref[...]
            m_new = jnp.maximum(m_old, jnp.max(sc, -1, keepdims=True))
            alpha = jnp.exp(m_old - m_new)
            pe = jnp.exp(sc - m_new)
            l_ref[...] = alpha * l_ref[...] + jnp.sum(pe, -1, keepdims=True)
            acc_ref[...] = alpha * acc_ref[...] + _dot(pe, vb)
            m_ref[...] = m_new
            return carry

        lax.fori_loop(0, (q0 + tq + tk - 1) // tk, sel_step, 0)
        o_sel = acc_ref[...] / l_ref[...]

        ws = pl.multiple_of(jnp.maximum(q0 - NSA_WINDOW, 0), tq)
        kw = kwin_ref[pl.ds(ws, wlen), gc]
        vw = vwin_ref[pl.ds(ws, wlen), gc]
        sw = _dot_nt(qs, kw) * scale
        kpos = ws + _iota((tq, wlen), 1)
        tpos = q0 + _iota((tq, wlen), 0)
        okw = (kpos <= tpos) & (kpos > tpos - NSA_WINDOW)
        pw = _softmax_rows(sw + tile8(jnp.where(okw, 0.0, NEG_INF)))
        o_win = _dot(pw, vw)

        for h in range(NSA_HG):
            hh = g * NSA_HG + h
            rs = slice(h * tq, (h + 1) * tq)
            c0 = SSM_HEADS + 3 * hh
            o = (gates[:, c0:c0 + 1] * o_cmp[rs] + gates[:, c0 + 1:c0 + 2] * o_sel[rs]
                 + gates[:, c0 + 2:c0 + 3] * o_win[rs])
            o_ref[:, hh * dh:(hh + 1) * dh] = o.astype(o_ref.dtype)


def _nsa_mixer(q, kv, small, kvc):
    bsz, seq, _ = q.shape
    tq = NSA_Q_TILE
    ncp = seq // NSA_CMP_STRIDE
    n_sel = seq // NSA_SEL_BLOCK
    assert seq % NSA_SEL_K_TILE == 0 and seq >= NSA_WINDOW + tq and n_sel <= LANES
    ci = np.arange(ncp)[:, None]
    sj = np.arange(LANES)[None, :]
    ov = np.clip(np.minimum(ci * NSA_CMP_STRIDE + NSA_CMP_BLOCK, sj * NSA_SEL_BLOCK + NSA_SEL_BLOCK)
                 - np.maximum(ci * NSA_CMP_STRIDE, sj * NSA_SEL_BLOCK), 0, None) / NSA_CMP_BLOCK
    ov = np.where((ci < ncp - 1) & (sj < n_sel), ov, 0.0)
    rows = NSA_HG * tq
    kvblk = lambda idx: pl.BlockSpec((None, seq, NSA_KV), lambda b, i: (b, 0, idx))
    return pl.pallas_call(
        functools.partial(_nsa_kernel, seq=seq),
        grid=(bsz, seq // tq),
        in_specs=[
            pl.BlockSpec((None, tq, NSA_WIDTH), lambda b, i: (b, i, 0)),
            pl.BlockSpec((None, 2 * NSA_GROUPS, ncp, NSA_HEAD_DIM), lambda b, i: (b, 0, 0, 0)),
            kvblk(2), kvblk(3), kvblk(4), kvblk(5),
            pl.BlockSpec((None, tq, LANES), lambda b, i: (b, i, 0)),
            pl.BlockSpec((ncp, LANES), lambda b, i: (0, 0)),
        ],
        out_specs=pl.BlockSpec((None, tq, NSA_WIDTH), lambda b, i: (b, i, 0)),
        out_shape=jax.ShapeDtypeStruct((bsz, seq, NSA_WIDTH), BF16),
        scratch_shapes=[pltpu.VMEM((rows, 1), F32), pltpu.VMEM((rows, 1), F32),
                        pltpu.VMEM((rows, NSA_HEAD_DIM), F32)],
        compiler_params=_cp("parallel", "arbitrary"),
        name="nsa_attention",
    )(q, kvc, kv, kv, kv, kv, small, jnp.asarray(ov, BF16))


def _mla_prep_kernel(c_ref, kr_ref, pos_ref, qnw_ref, kvnw_ref, inv_ref, cq_ref, ckv_ref, k1_ref, k2_ref,
                     cos_ref, sin_ref):
    def rms(v, w):
        return v * lax.rsqrt(jnp.mean(v * v, -1, keepdims=True) + RMS_EPS) * w

    cq_ref[...] = rms(c_ref[:, 0:MLA_Q_RANK], qnw_ref[...]).astype(BF16)
    ckv_ref[...] = rms(c_ref[:, MLA_Q_RANK:MLA_Q_RANK + MLA_KV_RANK], kvnw_ref[...]).astype(BF16)
    ang = pos_ref[...].astype(F32) * inv_ref[...]
    cos = jnp.cos(ang)
    sin = jnp.sin(ang)
    half = MLA_ROPE // 2
    t1 = kr_ref[:, 0:half]
    t2 = kr_ref[:, half:MLA_ROPE]
    k1_ref[...] = (t1 * cos - t2 * sin).astype(BF16)
    k2_ref[...] = (t1 * sin + t2 * cos).astype(BF16)
    cos_ref[...] = cos
    sin_ref[...] = sin


def _mla_prep(c, kr, pos, q_norm_w, kv_norm_w, tm=512):
    m = c.shape[0]
    half = MLA_ROPE // 2
    inv = (1.0 / (ROPE_THETA ** (jnp.arange(0, MLA_ROPE, 2, dtype=F32) / MLA_ROPE)))[None, :]
    row = lambda w: pl.BlockSpec((tm, w), lambda i: (i, 0))
    vec = lambda w: pl.BlockSpec((1, w), lambda i: (0, 0))
    return pl.pallas_call(
        _mla_prep_kernel,
        grid=(m // tm,),
        in_specs=[row(MLA_Q_RANK + MLA_KV_RANK), row(LANES), row(1), vec(MLA_Q_RANK), vec(MLA_KV_RANK), vec(half)],
        out_specs=[row(MLA_Q_RANK), row(MLA_KV_RANK), row(half), row(half), row(half), row(half)],
        out_shape=[jax.ShapeDtypeStruct((m, MLA_Q_RANK), BF16), jax.ShapeDtypeStruct((m, MLA_KV_RANK), BF16),
                   jax.ShapeDtypeStruct((m, half), BF16), jax.ShapeDtypeStruct((m, half), BF16),
                   jax.ShapeDtypeStruct((m, half), F32), jax.ShapeDtypeStruct((m, half), F32)],
        compiler_params=_cp("parallel"),
        name="mla_prep",
    )(c, kr, pos, q_norm_w[None, :], kv_norm_w[None, :], inv)


def _q_rope_kernel(t_ref, cos_ref, sin_ref, e_ref, r1_ref, r2_ref):
    hw = MLA_HEADS * MLA_ROPE // 2
    cos = _dot_f32_by_exact(cos_ref[...], e_ref[...])
    sin = _dot_f32_by_exact(sin_ref[...], e_ref[...])
    t1 = t_ref[:, 0:hw]
    t2 = t_ref[:, hw:2 * hw]
    r1_ref[...] = (t1 * cos - t2 * sin).astype(BF16)
    r2_ref[...] = (t1 * sin + t2 * cos).astype(BF16)


def _q_rope(t, cos, sin, tm=512):
    m = t.shape[0]
    half = MLA_ROPE // 2
    hw = MLA_HEADS * half
    e = (np.arange(hw)[None, :] % half == np.arange(half)[:, None])
    row = lambda w: pl.BlockSpec((tm, w), lambda i: (i, 0))
    return pl.pallas_call(
        _q_rope_kernel,
        grid=(m // tm,),
        in_specs=[row(2 * hw), row(half), row(half), pl.BlockSpec((half, hw), lambda i: (0, 0))],
        out_specs=[row(hw), row(hw)],
        out_shape=[jax.ShapeDtypeStruct((m, hw), BF16)] * 2,
        compiler_params=_cp("parallel"),
        name="mla_q_rope",
    )(t, cos, sin, jnp.asarray(e, BF16))


def _mla_attn_kernel(qn_ref, qr_ref, kn_ref, kr_ref, v_ref, o_ref, m_ref, l_ref, acc_ref):
    tq = tk = MLA_TILE
    scale = (MLA_NOPE + MLA_ROPE) ** -0.5
    qi = pl.program_id(2)
    q0 = qi * tq
    qn = qn_ref[...]
    qr = qr_ref[...]
    m_ref[...] = jnp.full(m_ref.shape, NEG_INF, F32)
    l_ref[...] = jnp.zeros(l_ref.shape, F32)
    acc_ref[...] = jnp.zeros(acc_ref.shape, F32)

    def step(kt, carry):
        k0 = pl.multiple_of(kt * tk, tk)
        s = (_dot_nt(qn, kn_ref[pl.ds(k0, tk), :]) + _dot_nt(qr, kr_ref[pl.ds(k0, tk), :])) * scale
        s = jnp.where(k0 + _iota((tq, tk), 1) <= q0 + _iota((tq, tk), 0), s, NEG_INF)
        m_old = m_ref[...]
        m_new = jnp.maximum(m_old, jnp.max(s, -1, keepdims=True))
        alpha = jnp.exp(m_old - m_new)
        p = jnp.exp(s - m_new)
        l_ref[...] = alpha * l_ref[...] + jnp.sum(p, -1, keepdims=True)
        acc_ref[...] = alpha * acc_ref[...] + _dot(p, v_ref[pl.ds(k0, tk), :])
        m_ref[...] = m_new
        return carry

    lax.fori_loop(0, qi + 1, step, 0)
    o_ref[...] = (acc_ref[...] / l_ref[...]).astype(o_ref.dtype)


def _mla_attention(qn, qr, kv, kr):
    bsz, seq, _ = qn.shape
    t = MLA_TILE
    return pl.pallas_call(
        _mla_attn_kernel,
        grid=(bsz, MLA_HEADS, seq // t),
        in_specs=[
            pl.BlockSpec((None, t, MLA_NOPE), lambda b, h, i: (b, i, h)),
            pl.BlockSpec((None, None, t, MLA_ROPE), lambda b, h, i: (b, h, i, 0)),
            pl.BlockSpec((None, seq, MLA_NOPE), lambda b, h, i: (b, 0, 2 * h)),
            pl.BlockSpec((None, seq, MLA_ROPE), lambda b, h, i: (b, 0, 0)),
            pl.BlockSpec((None, seq, MLA_V), lambda b, h, i: (b, 0, 2 * h + 1)),
        ],
        out_specs=pl.BlockSpec((None, t, MLA_V), lambda b, h, i: (b, i, h)),
        out_shape=jax.ShapeDtypeStruct((bsz, seq, MLA_HEADS * MLA_V), BF16),
        scratch_shapes=[pltpu.VMEM((t, 1), F32), pltpu.VMEM((t, 1), F32), pltpu.VMEM((t, MLA_V), F32)],
        compiler_params=_cp("parallel", "parallel", "arbitrary"),
        name="mla_attention",
    )(qn, qr, kv, kr, kv)


def _hgrn_kernel(hq_ref, hf_ref, hi_ref, hg_ref, lb_ref, nw_ref, o_ref, st):
    C = HGRN_CHUNK

    @pl.when(pl.program_id(2) == 0)
    def _():
        st[...] = jnp.zeros_like(st)

    lb = lb_ref[...]
    low = _iota((C, C), 1) <= _iota((C, C), 0)
    tri = jnp.where(low, 1.0, 0.0)
    for c in range(HGRN_TILE // C):
        rs = slice(c * C, (c + 1) * C)
        hf = hf_ref[rs, :]
        v = hi_ref[rs, :]
        q = _silu(hq_ref[rs, :])
        f = lb + (1.0 - lb) * jax.nn.sigmoid(hf)
        k = (1.0 - lb) * jax.nn.sigmoid(-hf)
        b = _dot_exact_by_f32(tri, jnp.log(f))
        b_last = b[C - 1:C, :]
        q_t = q * jnp.exp(b)
        k_t = k * jnp.exp(-b)
        k_end = k * jnp.exp(b_last - b)
        att = jnp.where(low, _dot_nt(q_t, k_t), 0.0)
        s_t = st[...]
        o = _dot(att, v) + _dot_nt(q_t, s_t)
        st[...] = s_t * jnp.exp(b_last) + _dot(v.T, k_end)
        o = o * lax.rsqrt(jnp.mean(o * o, -1, keepdims=True) + RMS_EPS) * nw_ref[...]
        o_ref[rs, :] = (o * _silu(hg_ref[rs, :])).astype(o_ref.dtype)


def _hgrn_mixer(hp, lb, norm_w):
    bsz, seq, _ = hp.shape
    t = HGRN_TILE
    part = lambda p: pl.BlockSpec((None, t, HGRN_KEY), lambda b, h, i: (b, i, p * HGRN_HEADS + h))
    return pl.pallas_call(
        _hgrn_kernel,
        grid=(bsz, HGRN_HEADS, seq // t),
        in_specs=[part(0), part(1), part(2), part(3),
                  pl.BlockSpec((None, 1, HGRN_KEY), lambda b, h, i: (h, 0, 0)),
                  pl.BlockSpec((1, HGRN_VAL), lambda b, h, i: (0, 0))],
        out_specs=pl.BlockSpec((None, t, HGRN_VAL), lambda b, h, i: (b, i, h)),
        out_shape=jax.ShapeDtypeStruct((bsz, seq, HGRN_WIDTH), BF16),
        scratch_shapes=[pltpu.VMEM((HGRN_VAL, HGRN_KEY), F32)],
        compiler_params=_cp("parallel", "parallel", "arbitrary"),
        name="hgrn2_mixer",
    )(hp, hp, hp, hp, lb.reshape(HGRN_HEADS, 1, HGRN_KEY), norm_w[None, :])


def _router_kernel(x_ref, w_ref, o_ref):
    x = x_ref[...]
    w = w_ref[...]
    xh = x.astype(BF16)
    xl = (x - xh.astype(F32)).astype(BF16)
    wh = w.astype(BF16)
    wl = (w - wh.astype(F32)).astype(BF16)
    d = lambda a, b: jnp.dot(a, b, preferred_element_type=F32)
    o_ref[...] = d(xh, wh) + d(xh, wl) + d(xl, wh)


def _router_logits(x, w_router, tm=512):
    m, d = x.shape
    w = _pad_cols(w_router, LANES)
    return pl.pallas_call(
        _router_kernel,
        grid=(m // tm,),
        in_specs=[pl.BlockSpec((tm, d), lambda i: (i, 0)), pl.BlockSpec((d, LANES), lambda i: (0, 0))],
        out_specs=pl.BlockSpec((tm, LANES), lambda i: (i, 0)),
        out_shape=jax.ShapeDtypeStruct((m, LANES), F32),
        compiler_params=_cp("parallel"),
        name="moe_router",
    )(x, w)


def _expert_kernel(be_ref, nv_ref, x_ref, wg_ref, wu_ref, wd_ref, rw_ref, o_ref):
    i = pl.program_id(0)

    @pl.when(i < nv_ref[0])
    def _():
        x = x_ref[...]
        h = _silu(jnp.dot(x, wg_ref[...], preferred_element_type=F32)) * jnp.dot(
            x, wu_ref[...], preferred_element_type=F32)
        o_ref[...] = jnp.dot(h.astype(BF16), wd_ref[...], preferred_element_type=F32) * rw_ref[...]

    @pl.when(i >= nv_ref[0])
    def _():
        o_ref[...] = jnp.zeros_like(o_ref)


def _expert_ffn(x_rows, blk_e, n_valid, w_gate, w_up, w_down, row_w):
    p, d = x_rows.shape
    mb = MOE_ROW_BLOCK
    ff = w_gate.shape[2]
    xmap = lambda i, be, nv: (jnp.minimum(i, nv[0] - 1), 0)
    wmap = lambda i, be, nv: (be[i], 0, 0)
    grid_spec = pltpu.PrefetchScalarGridSpec(
        num_scalar_prefetch=2,
        grid=(p // mb,),
        in_specs=[
            pl.BlockSpec((mb, d), xmap),
            pl.BlockSpec((None, d, ff), wmap), pl.BlockSpec((None, d, ff), wmap), pl.BlockSpec((None, ff, d), wmap),
            pl.BlockSpec((mb, 1), xmap),
        ],
        out_specs=pl.BlockSpec((mb, d), lambda i, be, nv: (i, 0)),
    )
    return pl.pallas_call(
        _expert_kernel,
        grid_spec=grid_spec,
        out_shape=jax.ShapeDtypeStruct((p, d), F32),
        compiler_params=_cp("arbitrary"),
        name="moe_expert_ffn",
    )(blk_e, n_valid, x_rows, w_gate, w_up, w_down, row_w)


def _moe_ffn(x, x_bf, w_router, router_bias, w_gate, w_up, w_down, ws_gate, ws_up, ws_down):
    T, d = x.shape
    E = N_EXPERTS
    M = MOE_ROW_BLOCK
    scores = jax.nn.sigmoid(_router_logits(x, w_router)[:, :E])
    biased = scores + router_bias.astype(F32)
    grp_score = lax.top_k(biased.reshape(T, N_EXPERT_GROUPS, E // N_EXPERT_GROUPS), 2)[0].sum(-1)
    _, top_g = lax.top_k(grp_score, TOP_GROUPS)
    gmask = jnp.any(top_g[..., None] == jnp.arange(N_EXPERT_GROUPS), axis=1)
    masked = jnp.where(jnp.repeat(gmask, E // N_EXPERT_GROUPS, axis=1), biased, -jnp.inf)
    _, top_e = lax.top_k(masked, TOP_K)
    w = jnp.take_along_axis(scores, top_e, axis=1)
    w = w / jnp.sum(w, -1, keepdims=True) * ROUTED_SCALE

    A = T * TOP_K
    NB = -(-A // M) + E
    P = NB * M
    flat_e = top_e.reshape(A).astype(jnp.int32)
    order = jnp.argsort(flat_e)
    se = flat_e[order]
    counts = jnp.zeros((E,), jnp.int32).at[flat_e].add(1)
    starts = jnp.cumsum(counts) - counts
    pcounts = (counts + M - 1) // M * M
    pends = jnp.cumsum(pcounts)
    pstarts = pends - pcounts
    dest = pstarts[se] + (jnp.arange(A, dtype=jnp.int32) - starts[se])
    row_tok = jnp.full((P,), T, jnp.int32).at[dest].set(order // TOP_K)
    row_w = jnp.zeros((P,), F32).at[dest].set(w.reshape(A)[order])
    blk_e = jnp.minimum(jnp.searchsorted(pends, jnp.arange(NB, dtype=jnp.int32) * M, side='right'),
                        E - 1).astype(jnp.int32)
    n_valid = (pends[-1] // M).astype(jnp.int32).reshape(1)
    slot = jnp.zeros((A,), jnp.int32).at[order].set(dest)

    x_pad = jnp.concatenate([x_bf, jnp.zeros((1, d), BF16)], axis=0)
    y_rows = _expert_ffn(x_pad[row_tok], blk_e, n_valid, w_gate.astype(BF16), w_up.astype(BF16),
                         w_down.astype(BF16), row_w[:, None])
    routed = y_rows[slot].reshape(T, TOP_K, d).sum(axis=1)

    ones_blocks = jnp.zeros((T // M,), jnp.int32)
    shared = _expert_ffn(x_bf, ones_blocks, jnp.full((1,), T // M, jnp.int32), ws_gate[None].astype(BF16),
                         ws_up[None].astype(BF16), ws_down[None].astype(BF16), jnp.ones((T, 1), F32))
    return shared, routed


def _mixer_ab(x_bf, bsz, seq, w_in, conv_w, conv_b, dt_bias, a_log, d_skip, norm_w, cmp_pos, cmp_w, w_out):
    T = bsz * seq
    o_zx = SSM_INNER + SSM_CONV_DIM
    o_dt = o_zx + SSM_HEADS
    o_q = o_dt + NSA_WIDTH
    o_kv = o_q + 6 * NSA_KV
    w_bf = w_in.astype(BF16)
    zx = _matmul(x_bf, w_bf[:, :o_zx], F32)
    w_small = _pad_cols(jnp.concatenate([w_bf[:, o_zx:o_dt], w_bf[:, o_kv:]], axis=1), LANES)
    small = _matmul(x_bf, w_small, F32)
    q = _matmul(x_bf, w_bf[:, o_dt:o_q], BF16)
    kv = _matmul(x_bf, w_bf[:, o_q:o_kv], BF16)

    zx3 = zx.reshape(bsz, seq, o_zx)
    small3 = small.reshape(bsz, seq, LANES)
    dt_t = jnp.swapaxes(small3[:, :, :SSM_HEADS], 1, 2)
    y_a = _ssd_mixer(zx3, small3, dt_t, conv_w, conv_b, dt_bias, a_log, d_skip, norm_w)

    kv3 = kv.reshape(bsz, seq, 6 * NSA_KV)
    ncp = seq // NSA_CMP_STRIDE
    strips = kv3[:, :, :2 * NSA_KV].reshape(bsz, ncp, NSA_CMP_STRIDE, 2 * NSA_GROUPS, NSA_HEAD_DIM)
    strips = strips.transpose(0, 3, 1, 2, 4).reshape(bsz, 2 * NSA_GROUPS, ncp, NSA_CMP_STRIDE * NSA_HEAD_DIM)
    kvc = _nsa_compress(strips, cmp_w, cmp_pos)
    y_b = _nsa_mixer(q.reshape(bsz, seq, NSA_WIDTH), kv3, small3, kvc)

    y = jnp.concatenate([y_a.reshape(T, SSM_INNER), y_b.reshape(T, NSA_WIDTH)], axis=1)
    return _matmul(y, w_out.astype(BF16), F32)


def _mixer_cd(x_bf, bsz, seq, positions, w_in, q_norm_w, w_uq, kv_norm_w, w_ukv, lb, hgrn_norm_w, w_out):
    T = bsz * seq
    o_c = MLA_Q_RANK + MLA_KV_RANK
    o_kr = o_c + MLA_ROPE
    half = MLA_ROPE // 2
    w_bf = w_in.astype(BF16)
    c = _matmul(x_bf, w_bf[:, :o_c], F32)
    kr = _matmul(x_bf, _pad_cols(w_bf[:, o_c:o_kr], LANES), F32)
    hp = _matmul(x_bf, w_bf[:, o_kr:], F32)

    cqn, ckvn, k1, k2, cos, sin = _mla_prep(c, kr, positions.reshape(T, 1), q_norm_w, kv_norm_w)
    wq = w_uq.astype(BF16).reshape(MLA_Q_RANK, MLA_HEADS, MLA_NOPE + MLA_ROPE)
    w_nope = wq[:, :, :MLA_NOPE].reshape(MLA_Q_RANK, MLA_HEADS * MLA_NOPE)
    w_rope = jnp.concatenate([wq[:, :, MLA_NOPE:MLA_NOPE + half].reshape(MLA_Q_RANK, MLA_HEADS * half),
                              wq[:, :, MLA_NOPE + half:].reshape(MLA_Q_RANK, MLA_HEADS * half)], axis=1)
    qn = _matmul(cqn, w_nope, BF16)
    r1, r2 = _q_rope(_matmul(cqn, w_rope, F32), cos, sin)
    qr = jnp.concatenate([r1.reshape(bsz, seq, MLA_HEADS, half), r2.reshape(bsz, seq, MLA_HEADS, half)], axis=-1)
    qr = qr.transpose(0, 2, 1, 3)
    kvu = _matmul(ckvn, w_ukv.astype(BF16), BF16)
    y_c = _mla_attention(qn.reshape(bsz, seq, -1), qr, kvu.reshape(bsz, seq, -1),
                         jnp.concatenate([k1, k2], axis=1).reshape(bsz, seq, MLA_ROPE))
    y_d = _hgrn_mixer(hp.reshape(bsz, seq, -1), lb, hgrn_norm_w)
    y = jnp.concatenate([y_c.reshape(T, -1), y_d.reshape(T, -1)], axis=1)
    return _matmul(y, w_out.astype(BF16), F32)


def kernel(x, positions, ab_w_in, ssm_conv_w, ssm_conv_b, ssm_dt_bias, ssm_a_log, ssm_d, ssm_norm_w, nsa_cmp_pos, nsa_cmp_w, ab_w_out, cd_w_in, mla_q_norm_w, mla_w_uq, mla_kv_norm_w, mla_w_ukv, hgrn_lb_logits, hgrn_norm_w, cd_w_out, ln_g, ln_b, moe_w_router, moe_router_bias, moe_w_gate, moe_w_up, moe_w_down, moe_shared_w_gate, moe_shared_w_up, moe_shared_w_down):
    bsz, seq, d = x.shape
    T = bsz * seq
    lb_all = jnp.cumsum(jax.nn.softmax(hgrn_lb_logits.astype(F32), axis=0), axis=0)
    lb_all = lb_all - lb_all[0]
    xf = x.reshape(T, d)
    x_bf = xf.astype(BF16)
    for l in range(DEPTH):
        i = l // 2
        if l % 2 == 0:
            mix = _mixer_ab(x_bf, bsz, seq, ab_w_in[i], ssm_conv_w[i], ssm_conv_b[i], ssm_dt_bias[i], ssm_a_log[i],
                            ssm_d[i], ssm_norm_w[i], nsa_cmp_pos[i], nsa_cmp_w[i], ab_w_out[i])
        else:
            mix = _mixer_cd(x_bf, bsz, seq, positions, cd_w_in[i], mla_q_norm_w[i], mla_w_uq[i], mla_kv_norm_w[i],
                            mla_w_ukv[i], lb_all[l], hgrn_norm_w[i], cd_w_out[i])
        xf, x_bf = _deepnorm_ln(xf, [mix], ln_g[l, 0], ln_b[l, 0])
        shared, routed = _moe_ffn(xf, x_bf, moe_w_router[l], moe_router_bias[l], moe_w_gate[l], moe_w_up[l],
                                  moe_w_down[l], moe_shared_w_gate[l], moe_shared_w_up[l], moe_shared_w_down[l])
        xf, x_bf = _deepnorm_ln(xf, [shared, routed], ln_g[l, 1], ln_b[l, 1])
    return xf.reshape(bsz, seq, d)
```

```python
import functools

import numpy as np
import jax
import jax.numpy as jnp
from jax import lax
from jax.experimental import pallas as pl
from jax.experimental.pallas import tpu as pltpu

F32 = jnp.float32
BF16 = jnp.bfloat16

D_MODEL = 2048
DEPTH = 2
DN_ALPHA = float((2 * DEPTH) ** 0.25)
LN_EPS = 1e-5
RMS_EPS = 1e-6
NEG_INF = -1e30

SSM_HEAD_DIM = 64
SSM_INNER = D_MODEL
SSM_HEADS = SSM_INNER // SSM_HEAD_DIM
SSM_STATE = 128
SSM_GROUPS = 4
SSM_CONV = 4
SSM_CHUNK = 128
SSM_BC = SSM_GROUPS * SSM_STATE
SSM_CONV_DIM = SSM_INNER + 2 * SSM_BC
SSM_GROUP_W = SSM_INNER // SSM_GROUPS

NSA_HEADS = 16
NSA_GROUPS = 2
NSA_HG = NSA_HEADS // NSA_GROUPS
NSA_HEAD_DIM = D_MODEL // NSA_HEADS
NSA_WIDTH = NSA_HEADS * NSA_HEAD_DIM
NSA_KV = NSA_GROUPS * NSA_HEAD_DIM
NSA_CMP_BLOCK = 32
NSA_CMP_STRIDE = 16
NSA_SEL_BLOCK = 64
NSA_TOP_N = 8
NSA_WINDOW = 512
NSA_FORCED = 1e6
NSA_Q_TILE = 128
NSA_SEL_K_TILE = 512

MLA_HEADS = 16
MLA_Q_RANK = 512
MLA_KV_RANK = 512
MLA_NOPE = 128
MLA_ROPE = 64
MLA_V = 128
ROPE_THETA = 10000.0
MLA_TILE = 512

HGRN_HEADS = 16
HGRN_KEY = 128
HGRN_VAL = 128
HGRN_CHUNK = 64
HGRN_FDIM = HGRN_HEADS * HGRN_KEY
HGRN_WIDTH = HGRN_HEADS * HGRN_VAL
HGRN_TILE = 512

N_EXPERTS = 64
TOP_K = 8
N_EXPERT_GROUPS = 8
TOP_GROUPS = 4
EXPERT_FF = 512
ROUTED_SCALE = 2.5
MOE_ROW_BLOCK = 256

LANES = 128
VMEM_LIMIT = 48 * 1024 * 1024
EXPERT_VMEM_LIMIT = 56 * 1024 * 1024


def _cp(*sem):
    return pltpu.CompilerParams(dimension_semantics=sem, vmem_limit_bytes=VMEM_LIMIT)


def _dot(a, b):
    return jnp.dot(a.astype(BF16), b.astype(BF16), preferred_element_type=F32)


def _dot_nt(a, b):
    return lax.dot_general(a.astype(BF16), b.astype(BF16), (((1,), (1,)), ((), ())),
                           preferred_element_type=F32)


def _split3(a):
    a1 = a.astype(BF16)
    r1 = a - a1.astype(F32)
    a2 = r1.astype(BF16)
    r2 = r1 - a2.astype(F32)
    return a1, a2, r2.astype(BF16)


def _dot_f32_by_exact(a, t):
    t = t.astype(BF16)
    return sum(jnp.dot(p, t, preferred_element_type=F32) for p in _split3(a))


def _dot_exact_by_f32(t, a):
    t = t.astype(BF16)
    return sum(jnp.dot(t, p, preferred_element_type=F32) for p in _split3(a))


def _silu(x):
    return x * jax.nn.sigmoid(x)


def _softplus(x):
    return jnp.maximum(x, 0.0) + jnp.log(1.0 + jnp.exp(-jnp.abs(x)))


def _iota(shape, dim):
    return lax.broadcasted_iota(jnp.int32, shape, dim)


def _mm_kernel(x_ref, w_ref, o_ref):
    o_ref[...] = jnp.dot(x_ref[...], w_ref[...], preferred_element_type=F32).astype(o_ref.dtype)


def _mm2_kernel(xa_ref, xb_ref, w_ref, o_ref):
    ka = xa_ref.shape[1]
    acc = jnp.dot(xa_ref[...], w_ref[0:ka, :], preferred_element_type=F32)
    acc = acc + jnp.dot(xb_ref[...], w_ref[ka:, :], preferred_element_type=F32)
    o_ref[...] = acc.astype(o_ref.dtype)


def _pick_tile(n, pref):
    if n <= pref:
        return n
    t = pref - pref % LANES
    while n % t:
        t -= LANES
    return t


def _matmul(x, w, out_dtype, tm=1024, tn=1024):
    m, k = x.shape
    n = w.shape[1]
    tm = _pick_tile(m, tm)
    tn = _pick_tile(n, tn)
    return pl.pallas_call(
        _mm_kernel,
        grid=(m // tm, n // tn),
        in_specs=[pl.BlockSpec((tm, k), lambda i, j: (i, 0)), pl.BlockSpec((k, tn), lambda i, j: (0, j))],
        out_specs=pl.BlockSpec((tm, tn), lambda i, j: (i, j)),
        out_shape=jax.ShapeDtypeStruct((m, n), out_dtype),
        compiler_params=_cp("parallel", "arbitrary"),
        name="matmul",
    )(x, w)


def _matmul_cat(xa, xb, w, out_dtype, tm=512, tn=1024):
    m, ka = xa.shape
    kb = xb.shape[1]
    n = w.shape[1]
    tm = _pick_tile(m, tm)
    tn = _pick_tile(n, tn)
    return pl.pallas_call(
        _mm2_kernel,
        grid=(m // tm, n // tn),
        in_specs=[pl.BlockSpec((tm, ka), lambda i, j: (i, 0)), pl.BlockSpec((tm, kb), lambda i, j: (i, 0)),
                  pl.BlockSpec((ka + kb, tn), lambda i, j: (0, j))],
        out_specs=pl.BlockSpec((tm, tn), lambda i, j: (i, j)),
        out_shape=jax.ShapeDtypeStruct((m, n), out_dtype),
        compiler_params=_cp("parallel", "arbitrary"),
        name="matmul_cat",
    )(xa, xb, w)


def _pad_cols(w, mult):
    n = w.shape[1]
    pad = (-n) % mult
    return jnp.pad(w, ((0, 0), (0, pad))) if pad else w


def _ln_combine_kernel(x_ref, sh_ref, y_ref, w_ref, g_ref, b_ref, o_ref, obf_ref):
    v = DN_ALPHA * x_ref[...] + sh_ref[...].astype(F32)
    w = w_ref[...]
    for k in range(TOP_K):
        v = v + w[:, k:k + 1] * y_ref[k].astype(F32)
    mu = jnp.mean(v, -1, keepdims=True)
    d = v - mu
    var = jnp.mean(d * d, -1, keepdims=True)
    y = d * lax.rsqrt(var + LN_EPS) * g_ref[...] + b_ref[...]
    o_ref[...] = y
    obf_ref[...] = y.astype(BF16)


def _deepnorm_ln_combine(x, shared, y_k, w_k, g, b, tm=128):
    m, d = x.shape
    row = pl.BlockSpec((tm, d), lambda i: (i, 0))
    vec = pl.BlockSpec((1, d), lambda i: (0, 0))
    return pl.pallas_call(
        _ln_combine_kernel,
        grid=(m // tm,),
        in_specs=[row, row, pl.BlockSpec((TOP_K, tm, d), lambda i: (0, i, 0)),
                  pl.BlockSpec((tm, TOP_K), lambda i: (i, 0)), vec, vec],
        out_specs=[row, row],
        out_shape=[jax.ShapeDtypeStruct((m, d), F32), jax.ShapeDtypeStruct((m, d), BF16)],
        compiler_params=_cp("parallel"),
        name="deepnorm_ln_combine",
    )(x, shared, y_k, w_k, g.reshape(1, d), b.reshape(1, d))


def _ln_kernel(*refs, n_add):
    x_ref = refs[0]
    add_refs = refs[1:1 + n_add]
    g_ref, b_ref, o_ref, obf_ref = refs[1 + n_add:]
    v = DN_ALPHA * x_ref[...]
    for r in add_refs:
        v = v + r[...].astype(F32)
    mu = jnp.mean(v, -1, keepdims=True)
    d = v - mu
    var = jnp.mean(d * d, -1, keepdims=True)
    y = d * lax.rsqrt(var + LN_EPS) * g_ref[...] + b_ref[...]
    o_ref[...] = y
    obf_ref[...] = y.astype(BF16)


def _deepnorm_ln(x, adds, g, b, tm=256):
    m, d = x.shape
    row = pl.BlockSpec((tm, d), lambda i: (i, 0))
    vec = pl.BlockSpec((1, d), lambda i: (0, 0))
    return pl.pallas_call(
        functools.partial(_ln_kernel, n_add=len(adds)),
        grid=(m // tm,),
        in_specs=[row] * (1 + len(adds)) + [vec, vec],
        out_specs=[row, row],
        out_shape=[jax.ShapeDtypeStruct((m, d), F32), jax.ShapeDtypeStruct((m, d), BF16)],
        compiler_params=_cp("parallel"),
        name="deepnorm_ln",
    )(x, *adds, g.reshape(1, d), b.reshape(1, d))


def _ssd_kernel(z_ref, xs_ref, bc_ref, dt_ref, dtt_ref, cwx_ref, cbx_ref, cwb_ref, cbb_ref, dtb_ref, dtbt_ref,
                alog_ref, alogt_ref, dexp_ref, nw_ref, e_ref, o_ref, extx, extb, st, y_acc):
    L = SSM_CHUNK
    halo = 8

    @pl.when(pl.program_id(1) == 0)
    def _():
        extx[0:halo, :] = jnp.zeros((halo, SSM_INNER), F32)
        extb[0:halo, :] = jnp.zeros((halo, 2 * SSM_BC), F32)
        st[...] = jnp.zeros_like(st)

    extx[halo:halo + L, :] = xs_ref[...]
    extb[halo:halo + L, :] = bc_ref[...]

    def conv(ext, w_ref, b_ref):
        acc = b_ref[...]
        for k in range(SSM_CONV):
            acc = acc + ext[pl.ds(halo - (SSM_CONV - 1) + k, L), :] * w_ref[k:k + 1, :]
        return acc

    xs = _silu(conv(extx, cwx_ref, cbx_ref))
    bc = _silu(conv(extb, cwb_ref, cbb_ref))
    extx[0:halo, :] = xs_ref[L - halo:L, :]
    extb[0:halo, :] = bc_ref[L - halo:L, :]

    li = _iota((L, L), 0)
    si = _iota((L, L), 1)
    low = si <= li
    dt = _softplus(dt_ref[:, 0:SSM_HEADS] + dtb_ref[...])
    adt = dt * (-jnp.exp(alog_ref[...]))
    a_cs = _dot_exact_by_f32(jnp.where(low, 1.0, 0.0), adt)
    dtt = _softplus(dtt_ref[...] + dtbt_ref[...])
    adtt = dtt * (-jnp.exp(alogt_ref[...]))
    a_cst = _dot_f32_by_exact(adtt, jnp.where(li <= si, 1.0, 0.0))

    e = e_ref[...]
    dt_e = _dot_f32_by_exact(dt, e)
    acs_e = _dot_f32_by_exact(a_cs, e)
    a_last = acs_e[L - 1:L, :]
    x_dt = xs * dt_e
    xw = (x_dt * jnp.exp(a_last - acs_e)).astype(BF16)
    ea = jnp.exp(acs_e)
    chunk_decay = jnp.exp(a_last)
    x_dt_b = x_dt.astype(BF16)
    lane = _iota((L, 2 * SSM_HEAD_DIM), 1)

    for g in range(SSM_GROUPS):
        gc = slice(g * SSM_GROUP_W, (g + 1) * SSM_GROUP_W)
        bm = bc[:, g * SSM_STATE:(g + 1) * SSM_STATE]
        cm = bc[:, SSM_BC + g * SSM_STATE:SSM_BC + (g + 1) * SSM_STATE]
        cb = _dot_nt(cm, bm)
        st_g = st[:, gc]
        y_acc[:, gc] = _dot(cm, st_g) * ea[:, gc]
        for pr in range(SSM_GROUP_W // (2 * SSM_HEAD_DIM)):
            h0 = (g * SSM_GROUP_W) // SSM_HEAD_DIM + 2 * pr
            cols = slice(h0 * SSM_HEAD_DIM, (h0 + 2) * SSM_HEAD_DIM)
            xr = x_dt_b[:, cols]
            res = []
            for h in (h0, h0 + 1):
                seg = a_cs[:, h:h + 1] - a_cst[h:h + 1, :]
                res.append(_dot(cb * jnp.where(low, jnp.exp(seg), 0.0), xr))
            y_acc[:, cols] += jnp.where(lane < SSM_HEAD_DIM, res[0], res[1])
        st[:, gc] = st_g * chunk_decay[:, gc] + _dot(bm.T, xw[:, gc])

    y = y_acc[...] + xs * dexp_ref[...]
    y = y * _silu(z_ref[...])
    var = jnp.mean(y * y, -1, keepdims=True)
    o_ref[...] = (y * lax.rsqrt(var + RMS_EPS) * nw_ref[...]).astype(o_ref.dtype)


def _ssd_mixer(zx, small, dt_t, conv_w, conv_b, dt_bias, a_log, d_skip, norm_w):
    bsz, seq, _ = zx.shape
    L = SSM_CHUNK
    nblk = SSM_INNER // (2 * SSM_BC)
    e = (np.arange(SSM_INNER)[None, :] // SSM_HEAD_DIM == np.arange(SSM_HEADS)[:, None])
    full = lambda shape: pl.BlockSpec(shape, lambda b, c: (0,) * len(shape))
    return pl.pallas_call(
        _ssd_kernel,
        grid=(bsz, seq // L),
        in_specs=[
            pl.BlockSpec((None, L, SSM_INNER), lambda b, c: (b, c, 0)),
            pl.BlockSpec((None, L, SSM_INNER), lambda b, c: (b, c, 1)),
            pl.BlockSpec((None, L, 2 * SSM_BC), lambda b, c: (b, c, 2 * nblk)),
            pl.BlockSpec((None, L, LANES), lambda b, c: (b, c, 0)),
            pl.BlockSpec((None, SSM_HEADS, L), lambda b, c: (b, 0, c)),
            full((SSM_CONV, SSM_INNER)), full((1, SSM_INNER)),
            full((SSM_CONV, 2 * SSM_BC)), full((1, 2 * SSM_BC)),
            full((1, SSM_HEADS)), full((SSM_HEADS, 1)), full((1, SSM_HEADS)), full((SSM_HEADS, 1)),
            full((1, SSM_INNER)), full((1, SSM_INNER)), full((SSM_HEADS, SSM_INNER)),
        ],
        out_specs=pl.BlockSpec((None, L, SSM_INNER), lambda b, c: (b, c, 0)),
        out_shape=jax.ShapeDtypeStruct((bsz, seq, SSM_INNER), BF16),
        scratch_shapes=[
            pltpu.VMEM((L + 8, SSM_INNER), F32), pltpu.VMEM((L + 8, 2 * SSM_BC), F32),
            pltpu.VMEM((SSM_STATE, SSM_INNER), F32), pltpu.VMEM((L, SSM_INNER), F32),
        ],
        compiler_params=_cp("arbitrary", "arbitrary"),
        name="ssd_mixer",
    )(zx, zx, zx, small, dt_t,
      conv_w[:, :SSM_INNER], conv_b[None, :SSM_INNER], conv_w[:, SSM_INNER:], conv_b[None, SSM_INNER:],
      dt_bias[None, :], dt_bias[:, None], a_log[None, :], a_log[:, None],
      jnp.repeat(d_skip, SSM_HEAD_DIM)[None, :], norm_w[None, :], jnp.asarray(e, BF16))


def _nsa_compress_kernel(r_ref, w_ref, pos_ref, o_ref):
    ncp = r_ref.shape[0]
    half = NSA_CMP_STRIDE * NSA_HEAD_DIM
    r = r_ref[...].astype(F32)
    top = _dot(r + pos_ref[0:1, :], w_ref[0:half, :])
    bot = _dot(r + pos_ref[1:2, :], w_ref[half:2 * half, :])
    kc = top + pltpu.roll(bot, ncp - 1, axis=0)
    o_ref[...] = jnp.where(_iota(kc.shape, 0) < ncp - 1, kc, 0.0).astype(o_ref.dtype)


def _nsa_compress(strips, cmp_w, cmp_pos):
    bsz, _, ncp, width = strips.shape
    return pl.pallas_call(
        _nsa_compress_kernel,
        grid=(bsz, 2 * NSA_GROUPS),
        in_specs=[
            pl.BlockSpec((None, None, ncp, width), lambda b, j: (b, j, 0, 0)),
            pl.BlockSpec((None, 2 * width, NSA_HEAD_DIM), lambda b, j: (j // NSA_GROUPS, 0, 0)),
            pl.BlockSpec((None, 2, width), lambda b, j: (j // NSA_GROUPS, 0, 0)),
        ],
        out_specs=pl.BlockSpec((None, None, ncp, NSA_HEAD_DIM), lambda b, j: (b, j, 0, 0)),
        out_shape=jax.ShapeDtypeStruct((bsz, 2 * NSA_GROUPS, ncp, NSA_HEAD_DIM), BF16),
        compiler_params=_cp("parallel", "arbitrary"),
        name="nsa_compress",
    )(strips, cmp_w.astype(BF16), cmp_pos.reshape(2, 2, width))


def _softmax_rows(s):
    m = jnp.max(s, -1, keepdims=True)
    p = jnp.exp(s - m)
    return p / jnp.sum(p, -1, keepdims=True)


def _nsa_kernel(q_ref, kvc_ref, ksel_ref, vsel_ref, kwin_ref, vwin_ref, gate_ref, ov_ref, o_ref,
                m_ref, l_ref, acc_ref, *, seq):
    tq = NSA_Q_TILE
    tk = NSA_SEL_K_TILE
    dh = NSA_HEAD_DIM
    rows = NSA_HG * tq
    ncp = seq // NSA_CMP_STRIDE
    n_cmp = ncp - 1
    n_sel = seq // NSA_SEL_BLOCK
    top_n = min(NSA_TOP_N, n_sel)
    wlen = NSA_WINDOW + tq
    scale = dh ** -0.5
    q0 = pl.program_id(1) * tq

    t_row = q0 + (_iota((rows, 1), 0) & (tq - 1))
    gates = jax.nn.sigmoid(gate_ref[...])
    tile8 = lambda a: jnp.concatenate([a] * NSA_HG, axis=0)

    for g in range(NSA_GROUPS):
        gc = slice(g * dh, (g + 1) * dh)
        qs = jnp.concatenate([q_ref[:, (g * NSA_HG + h) * dh:(g * NSA_HG + h + 1) * dh] for h in range(NSA_HG)],
                             axis=0)

        s = _dot_nt(qs, kvc_ref[g]) * scale
        ci = _iota((rows, ncp), 1)
        valid = (ci * NSA_CMP_STRIDE + (NSA_CMP_BLOCK - 1) <= t_row) & (ci < n_cmp)
        p = _softmax_rows(jnp.where(valid, s, NEG_INF))
        p = jnp.where(t_row >= NSA_CMP_BLOCK - 1, p, 0.0)
        o_cmp = _dot(p, kvc_ref[NSA_GROUPS + g])
        psum = p[0:tq]
        for h in range(1, NSA_HG):
            psum = psum + p[h * tq:(h + 1) * tq]

        imp = _dot_f32_by_exact(psum, ov_ref[...])
        tok = q0 + _iota((tq, LANES), 0)
        j = _iota((tq, LANES), 1)
        cur = tok >> 6
        forced = (j == 0) | (j == cur) | (j == cur - 1)
        imp = jnp.where(forced, NSA_FORCED, jnp.where(j * NSA_SEL_BLOCK <= tok, imp, -1.0))
        imp = jnp.where(j < n_sel, imp, -2.0)
        jf = j.astype(F32)
        sel = jnp.zeros((tq, LANES), F32)
        for _ in range(top_n):
            mx = jnp.max(imp, -1, keepdims=True)
            first = jnp.min(jnp.where(imp == mx, jf, 1e9), -1, keepdims=True)
            hit = jf == first
            sel = jnp.where(hit, 1.0, sel)
            imp = jnp.where(hit, -3.0, imp)
        selb = sel.astype(BF16)

        m_ref[...] = jnp.full((rows, 1), NEG_INF, F32)
        l_ref[...] = jnp.zeros((rows, 1), F32)
        acc_ref[...] = jnp.zeros((rows, dh), F32)

        def sel_step(kt, carry):
            k0 = pl.multiple_of(kt * tk, tk)
            kb = ksel_ref[pl.ds(k0, tk), gc]
            vb = vsel_ref[pl.ds(k0, tk), gc]
            sc = _dot_nt(qs, kb) * scale
            blk_of_key = (k0 + _iota((LANES, tk), 1)) >> 6
            expand = jnp.where(blk_of_key == _iota((LANES, tk), 0), 1.0, 0.0).astype(BF16)
            picked = jnp.dot(selb, expand, preferred_element_type=F32)
            ok = (picked > 0.5) & (k0 + _iota((tq, tk), 1) <= q0 + _iota((tq, tk), 0))
            sc = sc + tile8(jnp.where(ok, 0.0, NEG_INF))
            m_old = m_ref[...]
            m_new = jnp.maximum(m_old, jnp.max(sc, -1, keepdims=True))
            alpha = jnp.exp(m_old - m_new)
            pe = jnp.exp(sc - m_new)
            l_ref[...] = alpha * l_ref[...] + jnp.sum(pe, -1, keepdims=True)
            acc_ref[...] = alpha * acc_ref[...] + _dot(pe, vb)
            m_ref[...] = m_new
            return carry

        lax.fori_loop(0, (q0 + tq + tk - 1) // tk, sel_step, 0)
        o_sel = acc_ref[...] / l_ref[...]

        ws = pl.multiple_of(jnp.maximum(q0 - NSA_WINDOW, 0), tq)
        kw = kwin_ref[pl.ds(ws, wlen), gc]
        vw = vwin_ref[pl.ds(ws, wlen), gc]
        sw = _dot_nt(qs, kw) * scale
        kpos = ws + _iota((tq, wlen), 1)
        tpos = q0 + _iota((tq, wlen), 0)
        okw = (kpos <= tpos) & (kpos > tpos - NSA_WINDOW)
        pw = _softmax_rows(sw + tile8(jnp.where(okw, 0.0, NEG_INF)))
        o_win = _dot(pw, vw)

        for h in range(NSA_HG):
            hh = g * NSA_HG + h
            rs = slice(h * tq, (h + 1) * tq)
            c0 = SSM_HEADS + 3 * hh
            o = (gates[:, c0:c0 + 1] * o_cmp[rs] + gates[:, c0 + 1:c0 + 2] * o_sel[rs]
                 + gates[:, c0 + 2:c0 + 3] * o_win[rs])
            o_ref[:, hh * dh:(hh + 1) * dh] = o.astype(o_ref.dtype)


def _nsa_mixer(q, kv, small, kvc):
    bsz, seq, _ = q.shape
    tq = NSA_Q_TILE
    ncp = seq // NSA_CMP_STRIDE
    n_sel = seq // NSA_SEL_BLOCK
    assert seq % NSA_SEL_K_TILE == 0 and seq >= NSA_WINDOW + tq and n_sel <= LANES
    ci = np.arange(ncp)[:, None]
    sj = np.arange(LANES)[None, :]
    ov = np.clip(np.minimum(ci * NSA_CMP_STRIDE + NSA_CMP_BLOCK, sj * NSA_SEL_BLOCK + NSA_SEL_BLOCK)
                 - np.maximum(ci * NSA_CMP_STRIDE, sj * NSA_SEL_BLOCK), 0, None) / NSA_CMP_BLOCK
    ov = np.where((ci < ncp - 1) & (sj < n_sel), ov, 0.0)
    rows = NSA_HG * tq
    kvblk = lambda idx: pl.BlockSpec((None, seq, NSA_KV), lambda b, i: (b, 0, idx))
    return pl.pallas_call(
        functools.partial(_nsa_kernel, seq=seq),
        grid=(bsz, seq // tq),
        in_specs=[
            pl.BlockSpec((None, tq, NSA_WIDTH), lambda b, i: (b, i, 0)),
            pl.BlockSpec((None, 2 * NSA_GROUPS, ncp, NSA_HEAD_DIM), lambda b, i: (b, 0, 0, 0)),
            kvblk(2), kvblk(3), kvblk(4), kvblk(5),
            pl.BlockSpec((None, tq, LANES), lambda b, i: (b, i, 0)),
            pl.BlockSpec((ncp, LANES), lambda b, i: (0, 0)),
        ],
        out_specs=pl.BlockSpec((None, tq, NSA_WIDTH), lambda b, i: (b, i, 0)),
        out_shape=jax.ShapeDtypeStruct((bsz, seq, NSA_WIDTH), BF16),
        scratch_shapes=[pltpu.VMEM((rows, 1), F32), pltpu.VMEM((rows, 1), F32),
                        pltpu.VMEM((rows, NSA_HEAD_DIM), F32)],
        compiler_params=_cp("parallel", "arbitrary"),
        name="nsa_attention",
    )(q, kvc, kv, kv, kv, kv, small, jnp.asarray(ov, BF16))


def _mla_prep_kernel(c_ref, kr_ref, pos_ref, qnw_ref, kvnw_ref, inv_ref, cq_ref, ckv_ref, k1_ref, k2_ref,
                     cos_ref, sin_ref):
    def rms(v, w):
        return v * lax.rsqrt(jnp.mean(v * v, -1, keepdims=True) + RMS_EPS) * w

    cq_ref[...] = rms(c_ref[:, 0:MLA_Q_RANK], qnw_ref[...]).astype(BF16)
    ckv_ref[...] = rms(c_ref[:, MLA_Q_RANK:MLA_Q_RANK + MLA_KV_RANK], kvnw_ref[...]).astype(BF16)
    ang = pos_ref[...].astype(F32) * inv_ref[...]
    cos = jnp.cos(ang)
    sin = jnp.sin(ang)
    half = MLA_ROPE // 2
    t1 = kr_ref[:, 0:half]
    t2 = kr_ref[:, half:MLA_ROPE]
    k1_ref[...] = (t1 * cos - t2 * sin).astype(BF16)
    k2_ref[...] = (t1 * sin + t2 * cos).astype(BF16)
    cos_ref[...] = cos
    sin_ref[...] = sin


def _mla_prep(c, kr, pos, q_norm_w, kv_norm_w, tm=512):
    m = c.shape[0]
    half = MLA_ROPE // 2
    inv = (1.0 / (ROPE_THETA ** (jnp.arange(0, MLA_ROPE, 2, dtype=F32) / MLA_ROPE)))[None, :]
    row = lambda w: pl.BlockSpec((tm, w), lambda i: (i, 0))
    vec = lambda w: pl.BlockSpec((1, w), lambda i: (0, 0))
    return pl.pallas_call(
        _mla_prep_kernel,
        grid=(m // tm,),
        in_specs=[row(MLA_Q_RANK + MLA_KV_RANK), row(LANES), row(1), vec(MLA_Q_RANK), vec(MLA_KV_RANK), vec(half)],
        out_specs=[row(MLA_Q_RANK), row(MLA_KV_RANK), row(half), row(half), row(half), row(half)],
        out_shape=[jax.ShapeDtypeStruct((m, MLA_Q_RANK), BF16), jax.ShapeDtypeStruct((m, MLA_KV_RANK), BF16),
                   jax.ShapeDtypeStruct((m, half), BF16), jax.ShapeDtypeStruct((m, half), BF16),
                   jax.ShapeDtypeStruct((m, half), F32), jax.ShapeDtypeStruct((m, half), F32)],
        compiler_params=_cp("parallel"),
        name="mla_prep",
    )(c, kr, pos, q_norm_w[None, :], kv_norm_w[None, :], inv)


def _q_rope_kernel(t_ref, cos_ref, sin_ref, e_ref, r1_ref, r2_ref):
    hw = MLA_HEADS * MLA_ROPE // 2
    cos = _dot_f32_by_exact(cos_ref[...], e_ref[...])
    sin = _dot_f32_by_exact(sin_ref[...], e_ref[...])
    t1 = t_ref[:, 0:hw]
    t2 = t_ref[:, hw:2 * hw]
    r1_ref[...] = (t1 * cos - t2 * sin).astype(BF16)
    r2_ref[...] = (t1 * sin + t2 * cos).astype(BF16)


def _q_rope(t, cos, sin, tm=512):
    m = t.shape[0]
    half = MLA_ROPE // 2
    hw = MLA_HEADS * half
    e = (np.arange(hw)[None, :] % half == np.arange(half)[:, None])
    row = lambda w: pl.BlockSpec((tm, w), lambda i: (i, 0))
    return pl.pallas_call(
        _q_rope_kernel,
        grid=(m // tm,),
        in_specs=[row(2 * hw), row(half), row(half), pl.BlockSpec((half, hw), lambda i: (0, 0))],
        out_specs=[row(hw), row(hw)],
        out_shape=[jax.ShapeDtypeStruct((m, hw), BF16)] * 2,
        compiler_params=_cp("parallel"),
        name="mla_q_rope",
    )(t, cos, sin, jnp.asarray(e, BF16))


def _mla_attn_kernel(qn_ref, qr_ref, kn_ref, kr_ref, v_ref, o_ref, s_ref):
    t = MLA_TILE
    seq = qn_ref.shape[0]
    c = (MLA_NOPE + MLA_ROPE) ** -0.5 * float(np.log2(np.e))
    causal = _iota((t, t), 1) <= _iota((t, t), 0)
    for qi in range(seq // t):
        rows = slice(qi * t, (qi + 1) * t)
        qn = qn_ref[rows, :]
        qr = qr_ref[rows, :]
        for kt in range(qi + 1):
            keys = slice(kt * t, (kt + 1) * t)
            s = _dot_nt(qn, kn_ref[keys, :]) + _dot_nt(qr, kr_ref[keys, :])
            s_ref[:, keys] = jnp.where(causal, s, NEG_INF) if kt == qi else s
        n = (qi + 1) * t
        s = s_ref[:, 0:n]
        p = jnp.exp2((s - jnp.max(s, -1, keepdims=True)) * c)
        o = _dot(p, v_ref[0:n, :]) / jnp.sum(p, -1, keepdims=True)
        o_ref[rows, :] = o.astype(o_ref.dtype)


def _mla_attention(qn, qr, kv, kr):
    bsz, seq, _ = qn.shape
    assert seq % MLA_TILE == 0
    return pl.pallas_call(
        _mla_attn_kernel,
        grid=(bsz, MLA_HEADS),
        in_specs=[
            pl.BlockSpec((None, seq, MLA_NOPE), lambda b, h: (b, 0, h)),
            pl.BlockSpec((None, None, seq, MLA_ROPE), lambda b, h: (b, h, 0, 0)),
            pl.BlockSpec((None, seq, MLA_NOPE), lambda b, h: (b, 0, 2 * h)),
            pl.BlockSpec((None, seq, MLA_ROPE), lambda b, h: (b, 0, 0)),
            pl.BlockSpec((None, seq, MLA_V), lambda b, h: (b, 0, 2 * h + 1)),
        ],
        out_specs=pl.BlockSpec((None, seq, MLA_V), lambda b, h: (b, 0, h)),
        out_shape=jax.ShapeDtypeStruct((bsz, seq, MLA_HEADS * MLA_V), BF16),
        scratch_shapes=[pltpu.VMEM((MLA_TILE, seq), F32)],
        compiler_params=_cp("parallel", "arbitrary"),
        name="mla_attention",
    )(qn, qr, kv, kr, kv)


def _hgrn_kernel(hq_ref, hf_ref, hi_ref, hg_ref, lb_ref, nw_ref, o_ref, st):
    C = HGRN_CHUNK

    @pl.when(pl.program_id(2) == 0)
    def _():
        st[...] = jnp.zeros_like(st)

    lb = lb_ref[...]
    low = _iota((C, C), 1) <= _iota((C, C), 0)
    tri = jnp.where(low, 1.0, 0.0)
    for c in range(HGRN_TILE // C):
        rs = slice(c * C, (c + 1) * C)
        hf = hf_ref[rs, :]
        v = hi_ref[rs, :]
        q = _silu(hq_ref[rs, :])
        f = lb + (1.0 - lb) * jax.nn.sigmoid(hf)
        k = (1.0 - lb) * jax.nn.sigmoid(-hf)
        b = _dot_exact_by_f32(tri, jnp.log(f))
        b_last = b[C - 1:C, :]
        q_t = q * jnp.exp(b)
        k_t = k * jnp.exp(-b)
        k_end = k * jnp.exp(b_last - b)
        att = jnp.where(low, _dot_nt(q_t, k_t), 0.0)
        s_t = st[...]
        o = _dot(att, v) + _dot_nt(q_t, s_t)
        st[...] = s_t * jnp.exp(b_last) + _dot(v.T, k_end)
        o = o * lax.rsqrt(jnp.mean(o * o, -1, keepdims=True) + RMS_EPS) * nw_ref[...]
        o_ref[rs, :] = (o * _silu(hg_ref[rs, :])).astype(o_ref.dtype)


def _hgrn_mixer(hp, lb, norm_w):
    bsz, seq, _ = hp.shape
    t = HGRN_TILE
    part = lambda p: pl.BlockSpec((None, t, HGRN_KEY), lambda b, h, i: (b, i, p * HGRN_HEADS + h))
    return pl.pallas_call(
        _hgrn_kernel,
        grid=(bsz, HGRN_HEADS, seq // t),
        in_specs=[part(0), part(1), part(2), part(3),
                  pl.BlockSpec((None, 1, HGRN_KEY), lambda b, h, i: (h, 0, 0)),
                  pl.BlockSpec((1, HGRN_VAL), lambda b, h, i: (0, 0))],
        out_specs=pl.BlockSpec((None, t, HGRN_VAL), lambda b, h, i: (b, i, h)),
        out_shape=jax.ShapeDtypeStruct((bsz, seq, HGRN_WIDTH), BF16),
        scratch_shapes=[pltpu.VMEM((HGRN_VAL, HGRN_KEY), F32)],
        compiler_params=_cp("parallel", "parallel", "arbitrary"),
        name="hgrn2_mixer",
    )(hp, hp, hp, hp, lb.reshape(HGRN_HEADS, 1, HGRN_KEY), norm_w[None, :])


def _router_kernel(x_ref, wt_ref, bias_ref, esel_ref, rank_ref, wsel_ref, cnt_ref, carry):
    tm = x_ref.shape[0]
    n_grp = N_EXPERT_GROUPS
    gsz = N_EXPERTS // n_grp
    ninf = -jnp.inf

    @pl.when(pl.program_id(0) == 0)
    def _():
        carry[...] = jnp.zeros_like(carry)

    x = x_ref[...]
    w = wt_ref[...]
    xh = x.astype(BF16)
    xl = (x - xh.astype(F32)).astype(BF16)
    wh = w.astype(BF16)
    wl = (w - wh.astype(F32)).astype(BF16)
    dnt = lambda a, b: lax.dot_general(a, b, (((1,), (1,)), ((), ())), preferred_element_type=F32)
    scores = jax.nn.sigmoid(dnt(wh, xh) + dnt(wh, xl) + dnt(wl, xh))
    biased = scores + bias_ref[...]
    sub = _iota((gsz, tm), 0).astype(F32)

    def top1(blk):
        mx = jnp.max(blk, 0, keepdims=True)
        return mx, jnp.min(jnp.where(blk == mx, sub, 1e9), 0, keepdims=True)

    blocks = [biased[g * gsz:(g + 1) * gsz, :] for g in range(n_grp)]
    grp = []
    for blk in blocks:
        m1, i1 = top1(blk)
        m2, _ = top1(jnp.where(sub == i1, ninf, blk))
        grp.append(m1 + m2)
    grp = jnp.concatenate(grp, axis=0)
    gsel = jnp.zeros((n_grp, tm), F32)
    for _ in range(TOP_GROUPS):
        _, first = top1(grp)
        hit = sub == first
        gsel = jnp.where(hit, 1.0, gsel)
        grp = jnp.where(hit, ninf, grp)

    cand = [jnp.where(gsel[g:g + 1, :] > 0.5, blocks[g], ninf) for g in range(n_grp)]
    eid = [sub + float(g * gsz) for g in range(n_grp)]
    sel = [jnp.zeros((gsz, tm), F32) for _ in range(n_grp)]
    firsts = []
    for _ in range(TOP_K):
        mm = cand[0]
        for g in range(1, n_grp):
            mm = jnp.maximum(mm, cand[g])
        mx = jnp.max(mm, 0, keepdims=True)
        fm = jnp.where(cand[0] == mx, eid[0], 1e9)
        for g in range(1, n_grp):
            fm = jnp.minimum(fm, jnp.where(cand[g] == mx, eid[g], 1e9))
        first = jnp.min(fm, 0, keepdims=True)
        firsts.append(first)
        for g in range(n_grp):
            hit = eid[g] == first
            sel[g] = jnp.where(hit, 1.0, sel[g])
            cand[g] = jnp.where(hit, ninf, cand[g])

    selm = jnp.concatenate(sel, axis=0)
    before = jnp.where(_iota((tm, tm), 0) < _iota((tm, tm), 1), 1.0, 0.0).astype(BF16)
    rank = jnp.dot(selm.astype(BF16), before, preferred_element_type=F32) + carry[...]
    carry[...] = carry[...] + jnp.sum(selm, -1, keepdims=True)
    cnt_ref[...] = carry[...]

    ws, rs = [], []
    for first in firsts:
        wacc = jnp.zeros((gsz, tm), F32)
        racc = jnp.zeros((gsz, tm), F32)
        for g in range(n_grp):
            hit = eid[g] == first
            wacc = jnp.where(hit, scores[g * gsz:(g + 1) * gsz, :], wacc)
            racc = jnp.where(hit, rank[g * gsz:(g + 1) * gsz, :], racc)
        ws.append(jnp.sum(wacc, 0, keepdims=True))
        rs.append(jnp.sum(racc, 0, keepdims=True))
    wsum = ws[0]
    for wk in ws[1:]:
        wsum = wsum + wk
    esel_ref[...] = jnp.concatenate(firsts, axis=0).astype(jnp.int32)
    rank_ref[...] = jnp.concatenate(rs, axis=0).astype(jnp.int32)
    wsel_ref[...] = jnp.concatenate(ws, axis=0) / wsum * ROUTED_SCALE


def _route(x, w_router, router_bias, tm=256):
    T, d = x.shape
    E = N_EXPERTS
    assert N_EXPERT_GROUPS == E // N_EXPERT_GROUPS == 8
    kt = pl.BlockSpec((TOP_K, tm), lambda i: (0, i))
    return pl.pallas_call(
        _router_kernel,
        grid=(T // tm,),
        in_specs=[pl.BlockSpec((tm, d), lambda i: (i, 0)), pl.BlockSpec((E, d), lambda i: (0, 0)),
                  pl.BlockSpec((E, 1), lambda i: (0, 0))],
        out_specs=[kt, kt, kt, pl.BlockSpec((E, 1), lambda i: (0, 0))],
        out_shape=[jax.ShapeDtypeStruct((TOP_K, T), jnp.int32), jax.ShapeDtypeStruct((TOP_K, T), jnp.int32),
                   jax.ShapeDtypeStruct((TOP_K, T), F32), jax.ShapeDtypeStruct((E, 1), F32)],
        scratch_shapes=[pltpu.VMEM((E, 1), F32)],
        compiler_params=_cp("arbitrary"),
        name="moe_router",
    )(x, w_router.T, router_bias.astype(F32)[:, None])


def _expert_kernel(be_ref, nv_ref, x_ref, wg_ref, wu_ref, wd_ref, o_ref, wg_b, wu_b, wd_b):
    i = pl.program_id(0)

    @pl.when((i == 0) | (be_ref[i] != be_ref[jnp.maximum(i - 1, 0)]))
    def _():
        wg_b[...] = wg_ref[...].astype(BF16)
        wu_b[...] = wu_ref[...].astype(BF16)
        wd_b[...] = wd_ref[...].astype(BF16)

    @pl.when(i < nv_ref[0])
    def _():
        x = x_ref[...]
        h = _silu(jnp.dot(x, wg_b[...], preferred_element_type=F32)) * jnp.dot(
            x, wu_b[...], preferred_element_type=F32)
        o_ref[...] = jnp.dot(h.astype(BF16), wd_b[...], preferred_element_type=F32).astype(o_ref.dtype)

    @pl.when(i >= nv_ref[0])
    def _():
        o_ref[...] = jnp.zeros_like(o_ref)


def _expert_ffn(x_rows, blk_e, n_valid, w_gate, w_up, w_down, layer):
    p, d = x_rows.shape
    mb = MOE_ROW_BLOCK
    ff = w_gate.shape[3]
    xmap = lambda i, be, nv: (jnp.minimum(i, nv[0] - 1), 0)
    wmap = lambda i, be, nv: (layer, be[i], 0, 0)
    grid_spec = pltpu.PrefetchScalarGridSpec(
        num_scalar_prefetch=2,
        grid=(p // mb,),
        in_specs=[
            pl.BlockSpec((mb, d), xmap),
            pl.BlockSpec((None, None, d, ff), wmap), pl.BlockSpec((None, None, d, ff), wmap),
            pl.BlockSpec((None, None, ff, d), wmap),
        ],
        out_specs=pl.BlockSpec((mb, d), lambda i, be, nv: (i, 0)),
        scratch_shapes=[pltpu.VMEM((d, ff), BF16), pltpu.VMEM((d, ff), BF16), pltpu.VMEM((ff, d), BF16)],
    )
    return pl.pallas_call(
        _expert_kernel,
        grid_spec=grid_spec,
        out_shape=jax.ShapeDtypeStruct((p, d), BF16),
        compiler_params=pltpu.CompilerParams(dimension_semantics=("arbitrary",), vmem_limit_bytes=EXPERT_VMEM_LIMIT),
        name="moe_expert_ffn",
    )(blk_e, n_valid, x_rows, w_gate, w_up, w_down)


def _moe_ffn(x, x_bf, layer, w_router, router_bias, w_gate, w_up, w_down, ws_gate, ws_up, ws_down):
    T, d = x.shape
    E = N_EXPERTS
    M = MOE_ROW_BLOCK
    esel, rank, wsel, cnt = _route(x, w_router, router_bias)
    counts = cnt[:, 0].astype(jnp.int32)
    pcounts = (counts + M - 1) // M * M
    pends = jnp.cumsum(pcounts)
    pstarts = pends - pcounts
    NB = -(-(T * TOP_K) // M) + E
    blk_e = jnp.minimum(jnp.sum(pends[None, :] <= (jnp.arange(NB, dtype=jnp.int32) * M)[:, None], axis=1),
                        E - 1).astype(jnp.int32)
    n_valid = (pends[-1] // M).astype(jnp.int32).reshape(1)
    slot = rank + jnp.sum(jnp.where(esel[:, :, None] == jnp.arange(E, dtype=jnp.int32), pstarts, 0), axis=-1)
    slot = slot.reshape(TOP_K * T)
    row_tok = jnp.zeros((NB * M,), jnp.int32).at[slot].set(
        jnp.tile(jnp.arange(T, dtype=jnp.int32), TOP_K), unique_indices=True)
    y_rows = _expert_ffn(x_bf[row_tok], blk_e, n_valid, w_gate, w_up, w_down, layer)
    y_k = y_rows[slot].reshape(TOP_K, T, d)

    depth = ws_gate.shape[0]
    shared = _expert_ffn(x_bf, jnp.zeros((T // M,), jnp.int32), jnp.full((1,), T // M, jnp.int32),
                         ws_gate.reshape(depth, 1, d, -1), ws_up.reshape(depth, 1, d, -1),
                         ws_down.reshape(depth, 1, -1, d), layer)
    return shared, y_k, wsel.T


def _mixer_ab(x_bf, bsz, seq, w_in, conv_w, conv_b, dt_bias, a_log, d_skip, norm_w, cmp_pos, cmp_w, w_out):
    T = bsz * seq
    o_zx = SSM_INNER + SSM_CONV_DIM
    o_dt = o_zx + SSM_HEADS
    o_q = o_dt + NSA_WIDTH
    o_kv = o_q + 6 * NSA_KV
    w_bf = w_in.astype(BF16)
    zx = _matmul(x_bf, w_bf[:, :o_zx], F32)
    w_small = _pad_cols(jnp.concatenate([w_bf[:, o_zx:o_dt], w_bf[:, o_kv:]], axis=1), LANES)
    small = _matmul(x_bf, w_small, F32)
    q = _matmul(x_bf, w_bf[:, o_dt:o_q], BF16)
    kv = _matmul(x_bf, w_bf[:, o_q:o_kv], BF16)

    zx3 = zx.reshape(bsz, seq, o_zx)
    small3 = small.reshape(bsz, seq, LANES)
    dt_t = jnp.swapaxes(small3[:, :, :SSM_HEADS], 1, 2)
    y_a = _ssd_mixer(zx3, small3, dt_t, conv_w, conv_b, dt_bias, a_log, d_skip, norm_w)

    kv3 = kv.reshape(bsz, seq, 6 * NSA_KV)
    ncp = seq // NSA_CMP_STRIDE
    strips = kv3[:, :, :2 * NSA_KV].reshape(bsz, ncp, NSA_CMP_STRIDE, 2 * NSA_GROUPS, NSA_HEAD_DIM)
    strips = strips.transpose(0, 3, 1, 2, 4).reshape(bsz, 2 * NSA_GROUPS, ncp, NSA_CMP_STRIDE * NSA_HEAD_DIM)
    kvc = _nsa_compress(strips, cmp_w, cmp_pos)
    y_b = _nsa_mixer(q.reshape(bsz, seq, NSA_WIDTH), kv3, small3, kvc)

    return _matmul_cat(y_a.reshape(T, SSM_INNER), y_b.reshape(T, NSA_WIDTH), w_out.astype(BF16), F32)


def _mixer_cd(x_bf, bsz, seq, positions, w_in, q_norm_w, w_uq, kv_norm_w, w_ukv, lb, hgrn_norm_w, w_out):
    T = bsz * seq
    o_c = MLA_Q_RANK + MLA_KV_RANK
    o_kr = o_c + MLA_ROPE
    half = MLA_ROPE // 2
    w_bf = w_in.astype(BF16)
    c = _matmul(x_bf, w_bf[:, :o_c], F32)
    kr = _matmul(x_bf, _pad_cols(w_bf[:, o_c:o_kr], LANES), F32)
    hp = _matmul(x_bf, w_bf[:, o_kr:], F32)

    cqn, ckvn, k1, k2, cos, sin = _mla_prep(c, kr, positions.reshape(T, 1), q_norm_w, kv_norm_w)
    wq = w_uq.astype(BF16).reshape(MLA_Q_RANK, MLA_HEADS, MLA_NOPE + MLA_ROPE)
    w_nope = wq[:, :, :MLA_NOPE].reshape(MLA_Q_RANK, MLA_HEADS * MLA_NOPE)
    w_rope = jnp.concatenate([wq[:, :, MLA_NOPE:MLA_NOPE + half].reshape(MLA_Q_RANK, MLA_HEADS * half),
                              wq[:, :, MLA_NOPE + half:].reshape(MLA_Q_RANK, MLA_HEADS * half)], axis=1)
    qn = _matmul(cqn, w_nope, BF16)
    r1, r2 = _q_rope(_matmul(cqn, w_rope, F32), cos, sin)
    qr = jnp.concatenate([r1.reshape(bsz, seq, MLA_HEADS, half), r2.reshape(bsz, seq, MLA_HEADS, half)], axis=-1)
    qr = qr.transpose(0, 2, 1, 3)
    kvu = _matmul(ckvn, w_ukv.astype(BF16), BF16)
    y_c = _mla_attention(qn.reshape(bsz, seq, -1), qr, kvu.reshape(bsz, seq, -1),
                         jnp.concatenate([k1, k2], axis=1).reshape(bsz, seq, MLA_ROPE))
    y_d = _hgrn_mixer(hp.reshape(bsz, seq, -1), lb, hgrn_norm_w)
    return _matmul_cat(y_c.reshape(T, -1), y_d.reshape(T, -1), w_out.astype(BF16), F32)


def kernel(x, positions, ab_w_in, ssm_conv_w, ssm_conv_b, ssm_dt_bias, ssm_a_log, ssm_d, ssm_norm_w, nsa_cmp_pos, nsa_cmp_w, ab_w_out, cd_w_in, mla_q_norm_w, mla_w_uq, mla_kv_norm_w, mla_w_ukv, hgrn_lb_logits, hgrn_norm_w, cd_w_out, ln_g, ln_b, moe_w_router, moe_router_bias, moe_w_gate, moe_w_up, moe_w_down, moe_shared_w_gate, moe_shared_w_up, moe_shared_w_down):
    bsz, seq, d = x.shape
    T = bsz * seq
    lb_all = jnp.cumsum(jax.nn.softmax(hgrn_lb_logits.astype(F32), axis=0), axis=0)
    lb_all = lb_all - lb_all[0]
    xf = x.reshape(T, d)
    x_bf = xf.astype(BF16)
    for l in range(DEPTH):
        i = l // 2
        if l % 2 == 0:
            mix = _mixer_ab(x_bf, bsz, seq, ab_w_in[i], ssm_conv_w[i], ssm_conv_b[i], ssm_dt_bias[i], ssm_a_log[i],
                            ssm_d[i], ssm_norm_w[i], nsa_cmp_pos[i], nsa_cmp_w[i], ab_w_out[i])
        else:
            mix = _mixer_cd(x_bf, bsz, seq, positions, cd_w_in[i], mla_q_norm_w[i], mla_w_uq[i], mla_kv_norm_w[i],
                            mla_w_ukv[i], lb_all[l], hgrn_norm_w[i], cd_w_out[i])
        xf, x_bf = _deepnorm_ln(xf, [mix], ln_g[l, 0], ln_b[l, 0])
        shared, y_k, w_k = _moe_ffn(xf, x_bf, l, moe_w_router[l], moe_router_bias[l], moe_w_gate, moe_w_up,
                                    moe_w_down, moe_shared_w_gate, moe_shared_w_up, moe_shared_w_down)
        xf, x_bf = _deepnorm_ln_combine(xf, shared, y_k, w_k, ln_g[l, 1], ln_b[l, 1])
    return xf.reshape(bsz, seq, d)
```

```python
import functools

import numpy as np
import jax
import jax.numpy as jnp
from jax import lax
from jax.experimental import pallas as pl
from jax.experimental.pallas import tpu as pltpu

F32 = jnp.float32
BF16 = jnp.bfloat16

D_MODEL = 2048
DEPTH = 2
DN_ALPHA = float((2 * DEPTH) ** 0.25)
LN_EPS = 1e-5
RMS_EPS = 1e-6
NEG_INF = -1e30

SSM_HEAD_DIM = 64
SSM_INNER = D_MODEL
SSM_HEADS = SSM_INNER // SSM_HEAD_DIM
SSM_STATE = 128
SSM_GROUPS = 4
SSM_CONV = 4
SSM_CHUNK = 128
SSM_BC = SSM_GROUPS * SSM_STATE
SSM_CONV_DIM = SSM_INNER + 2 * SSM_BC
SSM_GROUP_W = SSM_INNER // SSM_GROUPS

NSA_HEADS = 16
NSA_GROUPS = 2
NSA_HG = NSA_HEADS // NSA_GROUPS
NSA_HEAD_DIM = D_MODEL // NSA_HEADS
NSA_WIDTH = NSA_HEADS * NSA_HEAD_DIM
NSA_KV = NSA_GROUPS * NSA_HEAD_DIM
NSA_CMP_BLOCK = 32
NSA_CMP_STRIDE = 16
NSA_SEL_BLOCK = 64
NSA_TOP_N = 8
NSA_WINDOW = 512
NSA_FORCED = 1e6
NSA_Q_TILE = 128
NSA_SEL_K_TILE = 512

MLA_HEADS = 16
MLA_Q_RANK = 512
MLA_KV_RANK = 512
MLA_NOPE = 128
MLA_ROPE = 64
MLA_V = 128
ROPE_THETA = 10000.0
MLA_TILE = 512

HGRN_HEADS = 16
HGRN_KEY = 128
HGRN_VAL = 128
HGRN_CHUNK = 64
HGRN_FDIM = HGRN_HEADS * HGRN_KEY
HGRN_WIDTH = HGRN_HEADS * HGRN_VAL
HGRN_TILE = 512
HGRN_HEADS_PER_STEP = 2

N_EXPERTS = 64
TOP_K = 8
N_EXPERT_GROUPS = 8
TOP_GROUPS = 4
EXPERT_FF = 512
ROUTED_SCALE = 2.5
MOE_ROW_BLOCK = 512
SHARED_ROW_BLOCK = 512

LANES = 128
VMEM_LIMIT = 48 * 1024 * 1024
EXPERT_VMEM_LIMIT = 56 * 1024 * 1024


def _cp(*sem):
    return pltpu.CompilerParams(dimension_semantics=sem, vmem_limit_bytes=VMEM_LIMIT)


def _dot(a, b):
    return jnp.dot(a.astype(BF16), b.astype(BF16), preferred_element_type=F32)


def _dot_nt(a, b):
    return lax.dot_general(a.astype(BF16), b.astype(BF16), (((1,), (1,)), ((), ())),
                           preferred_element_type=F32)


def _split3(a):
    a1 = a.astype(BF16)
    r1 = a - a1.astype(F32)
    a2 = r1.astype(BF16)
    r2 = r1 - a2.astype(F32)
    return a1, a2, r2.astype(BF16)


def _dot_f32_by_exact(a, t):
    t = t.astype(BF16)
    return sum(jnp.dot(p, t, preferred_element_type=F32) for p in _split3(a))


def _dot_exact_by_f32(t, a):
    t = t.astype(BF16)
    return sum(jnp.dot(t, p, preferred_element_type=F32) for p in _split3(a))


def _silu(x):
    return x * jax.nn.sigmoid(x)


def _softplus(x):
    return jnp.maximum(x, 0.0) + jnp.log(1.0 + jnp.exp(-jnp.abs(x)))


def _iota(shape, dim):
    return lax.broadcasted_iota(jnp.int32, shape, dim)


def _mm_kernel(x_ref, w_ref, o_ref):
    o_ref[...] = jnp.dot(x_ref[...], w_ref[...], preferred_element_type=F32).astype(o_ref.dtype)


def _mm2_kernel(xa_ref, xb_ref, w_ref, o_ref):
    ka = xa_ref.shape[1]
    acc = jnp.dot(xa_ref[...], w_ref[0:ka, :], preferred_element_type=F32)
    acc = acc + jnp.dot(xb_ref[...], w_ref[ka:, :], preferred_element_type=F32)
    o_ref[...] = acc.astype(o_ref.dtype)


def _pick_tile(n, pref):
    if n <= pref:
        return n
    t = pref - pref % LANES
    while n % t:
        t -= LANES
    return t


def _matmul(x, w, out_dtype, tm=2048, tn=512):
    m, k = x.shape
    n = w.shape[1]
    tm = _pick_tile(m, tm)
    tn = _pick_tile(n, tn)
    return pl.pallas_call(
        _mm_kernel,
        grid=(m // tm, n // tn),
        in_specs=[pl.BlockSpec((tm, k), lambda i, j: (i, 0)), pl.BlockSpec((k, tn), lambda i, j: (0, j))],
        out_specs=pl.BlockSpec((tm, tn), lambda i, j: (i, j)),
        out_shape=jax.ShapeDtypeStruct((m, n), out_dtype),
        compiler_params=_cp("parallel", "arbitrary"),
        name="matmul",
    )(x, w)


def _matmul_cat(xa, xb, w, out_dtype, tm=1024, tn=1024):
    m, ka = xa.shape
    kb = xb.shape[1]
    n = w.shape[1]
    tm = _pick_tile(m, tm)
    tn = _pick_tile(n, tn)
    return pl.pallas_call(
        _mm2_kernel,
        grid=(m // tm, n // tn),
        in_specs=[pl.BlockSpec((tm, ka), lambda i, j: (i, 0)), pl.BlockSpec((tm, kb), lambda i, j: (i, 0)),
                  pl.BlockSpec((ka + kb, tn), lambda i, j: (0, j))],
        out_specs=pl.BlockSpec((tm, tn), lambda i, j: (i, j)),
        out_shape=jax.ShapeDtypeStruct((m, n), out_dtype),
        compiler_params=_cp("parallel", "arbitrary"),
        name="matmul_cat",
    )(xa, xb, w)


def _pad_cols(w, mult):
    n = w.shape[1]
    pad = (-n) % mult
    return jnp.pad(w, ((0, 0), (0, pad))) if pad else w


def _ln_combine_kernel(x_ref, sh_ref, y_ref, w_ref, g_ref, b_ref, o_ref, obf_ref):
    half = x_ref.shape[1] // 2
    hi, lo = _unpack_bf16_pairs(sh_ref[...])
    w = w_ref[...]
    for k in range(TOP_K):
        yh, yl = _unpack_bf16_pairs(y_ref[k])
        hi = hi + w[:, k:k + 1] * yh
        lo = lo + w[:, k:k + 1] * yl
    v = DN_ALPHA * x_ref[...] + jnp.concatenate([hi, lo], axis=1)
    mu = jnp.mean(v, -1, keepdims=True)
    d = v - mu
    var = jnp.mean(d * d, -1, keepdims=True)
    y = d * lax.rsqrt(var + LN_EPS) * g_ref[...] + b_ref[...]
    o_ref[...] = y
    obf_ref[...] = y.astype(BF16)


def _deepnorm_ln_combine(x, shared, y_k, w_k, g, b, tm=128):
    m, d = x.shape
    row = pl.BlockSpec((tm, d), lambda i: (i, 0))
    vec = pl.BlockSpec((1, d), lambda i: (0, 0))
    return pl.pallas_call(
        _ln_combine_kernel,
        grid=(m // tm,),
        in_specs=[row, pl.BlockSpec((tm, d // 2), lambda i: (i, 0)),
                  pl.BlockSpec((TOP_K, tm, d // 2), lambda i: (0, i, 0)),
                  pl.BlockSpec((tm, TOP_K), lambda i: (i, 0)), vec, vec],
        out_specs=[row, row],
        out_shape=[jax.ShapeDtypeStruct((m, d), F32), jax.ShapeDtypeStruct((m, d), BF16)],
        compiler_params=_cp("parallel"),
        name="deepnorm_ln_combine",
    )(x, shared, y_k, w_k, g.reshape(1, d), b.reshape(1, d))


def _ln_kernel(*refs, n_add):
    x_ref = refs[0]
    add_refs = refs[1:1 + n_add]
    g_ref, b_ref, o_ref, obf_ref = refs[1 + n_add:]
    v = DN_ALPHA * x_ref[...]
    for r in add_refs:
        v = v + r[...].astype(F32)
    mu = jnp.mean(v, -1, keepdims=True)
    d = v - mu
    var = jnp.mean(d * d, -1, keepdims=True)
    y = d * lax.rsqrt(var + LN_EPS) * g_ref[...] + b_ref[...]
    o_ref[...] = y
    obf_ref[...] = y.astype(BF16)


def _deepnorm_ln(x, adds, g, b, tm=256):
    m, d = x.shape
    row = pl.BlockSpec((tm, d), lambda i: (i, 0))
    vec = pl.BlockSpec((1, d), lambda i: (0, 0))
    return pl.pallas_call(
        functools.partial(_ln_kernel, n_add=len(adds)),
        grid=(m // tm,),
        in_specs=[row] * (1 + len(adds)) + [vec, vec],
        out_specs=[row, row],
        out_shape=[jax.ShapeDtypeStruct((m, d), F32), jax.ShapeDtypeStruct((m, d), BF16)],
        compiler_params=_cp("parallel"),
        name="deepnorm_ln",
    )(x, *adds, g.reshape(1, d), b.reshape(1, d))


def _ssd_kernel(z_ref, xs_ref, bc_ref, dt_ref, dtt_ref, cwx_ref, cbx_ref, cwb_ref, cbb_ref, dtb_ref, dtbt_ref,
                alog_ref, alogt_ref, dexp_ref, nw_ref, e_ref, o_ref, extx, extb, st, y_acc):
    L = SSM_CHUNK
    halo = 8

    @pl.when(pl.program_id(1) == 0)
    def _():
        extx[0:halo, :] = jnp.zeros((halo, SSM_INNER), F32)
        extb[0:halo, :] = jnp.zeros((halo, 2 * SSM_BC), F32)
        st[...] = jnp.zeros_like(st)

    extx[halo:halo + L, :] = xs_ref[...]
    extb[halo:halo + L, :] = bc_ref[...]

    def conv(ext, w_ref, b_ref):
        acc = b_ref[...]
        for k in range(SSM_CONV):
            acc = acc + ext[pl.ds(halo - (SSM_CONV - 1) + k, L), :] * w_ref[k:k + 1, :]
        return acc

    xs = _silu(conv(extx, cwx_ref, cbx_ref))
    bc = _silu(conv(extb, cwb_ref, cbb_ref))
    extx[0:halo, :] = xs_ref[L - halo:L, :]
    extb[0:halo, :] = bc_ref[L - halo:L, :]

    li = _iota((L, L), 0)
    si = _iota((L, L), 1)
    low = si <= li
    dt = _softplus(dt_ref[:, 0:SSM_HEADS] + dtb_ref[...])
    adt = dt * (-jnp.exp(alog_ref[...]))
    a_cs = _dot_exact_by_f32(jnp.where(low, 1.0, 0.0), adt)
    dtt = _softplus(dtt_ref[...] + dtbt_ref[...])
    adtt = dtt * (-jnp.exp(alogt_ref[...]))
    a_cst = _dot_f32_by_exact(adtt, jnp.where(li <= si, 1.0, 0.0))

    e = e_ref[...]
    dt_e = _dot_f32_by_exact(dt, e)
    acs_e = _dot_f32_by_exact(a_cs, e)
    a_last = acs_e[L - 1:L, :]
    x_dt = xs * dt_e
    xw = (x_dt * jnp.exp(a_last - acs_e)).astype(BF16)
    ea = jnp.exp(acs_e)
    chunk_decay = jnp.exp(a_last)
    x_dt_b = x_dt.astype(BF16)
    lane = _iota((L, 2 * SSM_HEAD_DIM), 1)

    for g in range(SSM_GROUPS):
        gc = slice(g * SSM_GROUP_W, (g + 1) * SSM_GROUP_W)
        bm = bc[:, g * SSM_STATE:(g + 1) * SSM_STATE]
        cm = bc[:, SSM_BC + g * SSM_STATE:SSM_BC + (g + 1) * SSM_STATE]
        cb = _dot_nt(cm, bm)
        st_g = st[:, gc]
        y_acc[:, gc] = _dot(cm, st_g) * ea[:, gc]
        for pr in range(SSM_GROUP_W // (2 * SSM_HEAD_DIM)):
            h0 = (g * SSM_GROUP_W) // SSM_HEAD_DIM + 2 * pr
            cols = slice(h0 * SSM_HEAD_DIM, (h0 + 2) * SSM_HEAD_DIM)
            xr = x_dt_b[:, cols]
            res = []
            for h in (h0, h0 + 1):
                seg = a_cs[:, h:h + 1] - a_cst[h:h + 1, :]
                res.append(_dot(cb * jnp.where(low, jnp.exp(seg), 0.0), xr))
            y_acc[:, cols] += jnp.where(lane < SSM_HEAD_DIM, res[0], res[1])
        st[:, gc] = st_g * chunk_decay[:, gc] + _dot(bm.T, xw[:, gc])

    y = y_acc[...] + xs * dexp_ref[...]
    y = y * _silu(z_ref[...])
    var = jnp.mean(y * y, -1, keepdims=True)
    o_ref[...] = (y * lax.rsqrt(var + RMS_EPS) * nw_ref[...]).astype(o_ref.dtype)


def _ssd_mixer(zx, small, dt_t, conv_w, conv_b, dt_bias, a_log, d_skip, norm_w):
    bsz, seq, _ = zx.shape
    L = SSM_CHUNK
    nblk = SSM_INNER // (2 * SSM_BC)
    e = (np.arange(SSM_INNER)[None, :] // SSM_HEAD_DIM == np.arange(SSM_HEADS)[:, None])
    full = lambda shape: pl.BlockSpec(shape, lambda b, c: (0,) * len(shape))
    return pl.pallas_call(
        _ssd_kernel,
        grid=(bsz, seq // L),
        in_specs=[
            pl.BlockSpec((None, L, SSM_INNER), lambda b, c: (b, c, 0)),
            pl.BlockSpec((None, L, SSM_INNER), lambda b, c: (b, c, 1)),
            pl.BlockSpec((None, L, 2 * SSM_BC), lambda b, c: (b, c, 2 * nblk)),
            pl.BlockSpec((None, L, LANES), lambda b, c: (b, c, 0)),
            pl.BlockSpec((None, SSM_HEADS, L), lambda b, c: (b, 0, c)),
            full((SSM_CONV, SSM_INNER)), full((1, SSM_INNER)),
            full((SSM_CONV, 2 * SSM_BC)), full((1, 2 * SSM_BC)),
            full((1, SSM_HEADS)), full((SSM_HEADS, 1)), full((1, SSM_HEADS)), full((SSM_HEADS, 1)),
            full((1, SSM_INNER)), full((1, SSM_INNER)), full((SSM_HEADS, SSM_INNER)),
        ],
        out_specs=pl.BlockSpec((None, L, SSM_INNER), lambda b, c: (b, c, 0)),
        out_shape=jax.ShapeDtypeStruct((bsz, seq, SSM_INNER), BF16),
        scratch_shapes=[
            pltpu.VMEM((L + 8, SSM_INNER), F32), pltpu.VMEM((L + 8, 2 * SSM_BC), F32),
            pltpu.VMEM((SSM_STATE, SSM_INNER), F32), pltpu.VMEM((L, SSM_INNER), F32),
        ],
        compiler_params=_cp("arbitrary", "arbitrary"),
        name="ssd_mixer",
    )(zx, zx, zx, small, dt_t,
      conv_w[:, :SSM_INNER], conv_b[None, :SSM_INNER], conv_w[:, SSM_INNER:], conv_b[None, SSM_INNER:],
      dt_bias[None, :], dt_bias[:, None], a_log[None, :], a_log[:, None],
      jnp.repeat(d_skip, SSM_HEAD_DIM)[None, :], norm_w[None, :], jnp.asarray(e, BF16))


def _nsa_compress_kernel(r_ref, w_ref, pos_ref, o_ref):
    ncp = r_ref.shape[0]
    half = NSA_CMP_STRIDE * NSA_HEAD_DIM
    r = r_ref[...].astype(F32)
    top = _dot(r + pos_ref[0:1, :], w_ref[0:half, :])
    bot = _dot(r + pos_ref[1:2, :], w_ref[half:2 * half, :])
    kc = top + pltpu.roll(bot, ncp - 1, axis=0)
    o_ref[...] = jnp.where(_iota(kc.shape, 0) < ncp - 1, kc, 0.0).astype(o_ref.dtype)


def _nsa_compress(strips, cmp_w, cmp_pos):
    bsz, _, ncp, width = strips.shape
    return pl.pallas_call(
        _nsa_compress_kernel,
        grid=(bsz, 2 * NSA_GROUPS),
        in_specs=[
            pl.BlockSpec((None, None, ncp, width), lambda b, j: (b, j, 0, 0)),
            pl.BlockSpec((None, 2 * width, NSA_HEAD_DIM), lambda b, j: (j // NSA_GROUPS, 0, 0)),
            pl.BlockSpec((None, 2, width), lambda b, j: (j // NSA_GROUPS, 0, 0)),
        ],
        out_specs=pl.BlockSpec((None, None, ncp, NSA_HEAD_DIM), lambda b, j: (b, j, 0, 0)),
        out_shape=jax.ShapeDtypeStruct((bsz, 2 * NSA_GROUPS, ncp, NSA_HEAD_DIM), BF16),
        compiler_params=_cp("parallel", "arbitrary"),
        name="nsa_compress",
    )(strips, cmp_w.astype(BF16), cmp_pos.reshape(2, 2, width))


def _nsa_kernel(q_ref, kvc_ref, ksel_ref, vsel_ref, kwin_ref, vwin_ref, gate_ref, ovt_ref, o_ref,
                s_ref, p_ref, bias_ref, m_ref, l_ref, acc_ref, ocmp_ref, *, seq):
    tq = NSA_Q_TILE
    tk = NSA_SEL_K_TILE
    dh = NSA_HEAD_DIM
    ncp = seq // NSA_CMP_STRIDE
    n_cmp = ncp - 1
    n_sel = seq // NSA_SEL_BLOCK
    top_n = min(NSA_TOP_N, n_sel)
    wlen = NSA_WINDOW + tq
    c = dh ** -0.5 * float(np.log2(np.e))
    q0 = pl.program_id(1) * tq
    head_rows = [slice(h * tq, (h + 1) * tq) for h in range(NSA_HG)]

    gates = jax.nn.sigmoid(gate_ref[...])
    has_cmp = q0 + _iota((tq, 1), 0) >= NSA_CMP_BLOCK - 1
    ci = _iota((tq, ncp), 1)
    cmp_ok = (ci * NSA_CMP_STRIDE + (NSA_CMP_BLOCK - 1) <= q0 + _iota((tq, ncp), 0)) & (ci < n_cmp)
    cmp_bias = jnp.where(cmp_ok, 0.0, NEG_INF)

    groups = range(NSA_GROUPS)
    gcs = [slice(g * dh, (g + 1) * dh) for g in groups]
    qss = [jnp.concatenate([q_ref[:, (g * NSA_HG + h) * dh:(g * NSA_HG + h + 1) * dh] for h in range(NSA_HG)],
                           axis=0) for g in groups]
    selbs = []
    for g in groups:
        s_ref[g, :, 0:ncp] = _dot_nt(qss[g], kvc_ref[g])
        psum = jnp.zeros((tq, ncp), F32)
        for rs in head_rows:
            sc = s_ref[g, rs, 0:ncp] + cmp_bias
            pe = jnp.exp2((sc - jnp.max(sc, -1, keepdims=True)) * c)
            p = jnp.where(has_cmp, pe / jnp.sum(pe, -1, keepdims=True), 0.0)
            psum = psum + p
            p_ref[g, rs, 0:ncp] = p.astype(BF16)
        ocmp_ref[g] = _dot(p_ref[g, :, 0:ncp], kvc_ref[NSA_GROUPS + g])

        ovt = ovt_ref[...]
        imp = sum(lax.dot_general(ovt, part, (((1,), (1,)), ((), ())), preferred_element_type=F32)
                  for part in _split3(psum))
        j = _iota((LANES, tq), 0)
        tok = q0 + _iota((LANES, tq), 1)
        cur = tok >> 6
        forced = (j == 0) | (j == cur) | (j == cur - 1)
        imp = jnp.where(forced, NSA_FORCED, jnp.where(j * NSA_SEL_BLOCK <= tok, imp, -1.0))
        imp = jnp.where(j < n_sel, imp, -2.0)
        jf = j.astype(F32)
        sel = jnp.zeros((LANES, tq), F32)
        for _ in range(top_n):
            mx = jnp.max(imp, 0, keepdims=True)
            first = jnp.min(jnp.where(imp == mx, jf, 1e9), 0, keepdims=True)
            hit = jf == first
            sel = jnp.where(hit, 1.0, sel)
            imp = jnp.where(hit, -3.0, imp)
        selbs.append(sel.T.astype(BF16))

    m_ref[...] = jnp.full(m_ref.shape, NEG_INF, F32)
    l_ref[...] = jnp.zeros(l_ref.shape, F32)
    acc_ref[...] = jnp.zeros(acc_ref.shape, F32)

    def sel_step(kt, carry):
        k0 = pl.multiple_of(kt * tk, tk)
        blk_of_key = (k0 + _iota((LANES, tk), 1)) >> 6
        expand = jnp.where(blk_of_key == _iota((LANES, tk), 0), 1.0, 0.0).astype(BF16)
        causal = k0 + _iota((tq, tk), 1) <= q0 + _iota((tq, tk), 0)
        for g in groups:
            s_ref[g, :, 0:tk] = _dot_nt(qss[g], ksel_ref[pl.ds(k0, tk), gcs[g]])
            picked = jnp.dot(selbs[g], expand, preferred_element_type=F32)
            bias_ref[g, :, 0:tk] = jnp.where((picked > 0.5) & causal, 0.0, NEG_INF)
            for rs in head_rows:
                sc = s_ref[g, rs, 0:tk] + bias_ref[g, :, 0:tk]
                m_old = m_ref[g, rs, :]
                m_new = jnp.maximum(m_old, jnp.max(sc, -1, keepdims=True))
                alpha = jnp.exp2((m_old - m_new) * c)
                pe = jnp.exp2((sc - m_new) * c)
                l_ref[g, rs, :] = alpha * l_ref[g, rs, :] + jnp.sum(pe, -1, keepdims=True)
                acc_ref[g, rs, :] = alpha * acc_ref[g, rs, :]
                m_ref[g, rs, :] = m_new
                p_ref[g, rs, 0:tk] = pe.astype(BF16)
            acc_ref[g] += _dot(p_ref[g, :, 0:tk], vsel_ref[pl.ds(k0, tk), gcs[g]])
        return carry

    lax.fori_loop(0, (q0 + tq + tk - 1) // tk, sel_step, 0)

    ws = pl.multiple_of(jnp.maximum(q0 - NSA_WINDOW, 0), tq)
    kpos = ws + _iota((tq, wlen), 1)
    tpos = q0 + _iota((tq, wlen), 0)
    win_bias = jnp.where((kpos <= tpos) & (kpos > tpos - NSA_WINDOW), 0.0, NEG_INF)
    for g in groups:
        o_sel = acc_ref[g] / l_ref[g]
        s_ref[g, :, 0:wlen] = _dot_nt(qss[g], kwin_ref[pl.ds(ws, wlen), gcs[g]])
        for rs in head_rows:
            sc = s_ref[g, rs, 0:wlen] + win_bias
            pe = jnp.exp2((sc - jnp.max(sc, -1, keepdims=True)) * c)
            l_ref[g, rs, :] = jnp.sum(pe, -1, keepdims=True)
            p_ref[g, rs, 0:wlen] = pe.astype(BF16)
        o_win = _dot(p_ref[g, :, 0:wlen], vwin_ref[pl.ds(ws, wlen), gcs[g]]) / l_ref[g]
        for h, rs in enumerate(head_rows):
            hh = g * NSA_HG + h
            c0 = SSM_HEADS + 3 * hh
            o = (gates[:, c0:c0 + 1] * ocmp_ref[g, rs, :] + gates[:, c0 + 1:c0 + 2] * o_sel[rs]
                 + gates[:, c0 + 2:c0 + 3] * o_win[rs])
            o_ref[:, hh * dh:(hh + 1) * dh] = o.astype(o_ref.dtype)


def _nsa_mixer(q, kv, small, kvc):
    bsz, seq, _ = q.shape
    tq = NSA_Q_TILE
    ncp = seq // NSA_CMP_STRIDE
    n_sel = seq // NSA_SEL_BLOCK
    wlen = NSA_WINDOW + tq
    assert seq % NSA_SEL_K_TILE == 0 and seq >= wlen and n_sel <= LANES and tq == LANES
    swidth = max(wlen, NSA_SEL_K_TILE, ncp)
    sj = np.arange(LANES)[:, None]
    ci = np.arange(ncp)[None, :]
    ovt = np.clip(np.minimum(ci * NSA_CMP_STRIDE + NSA_CMP_BLOCK, sj * NSA_SEL_BLOCK + NSA_SEL_BLOCK)
                  - np.maximum(ci * NSA_CMP_STRIDE, sj * NSA_SEL_BLOCK), 0, None) / NSA_CMP_BLOCK
    ovt = np.where((ci < ncp - 1) & (sj < n_sel), ovt, 0.0)
    rows = NSA_HG * tq
    kvblk = lambda idx: pl.BlockSpec((None, seq, NSA_KV), lambda b, i: (b, 0, idx))
    return pl.pallas_call(
        functools.partial(_nsa_kernel, seq=seq),
        grid=(bsz, seq // tq),
        in_specs=[
            pl.BlockSpec((None, tq, NSA_WIDTH), lambda b, i: (b, i, 0)),
            pl.BlockSpec((None, 2 * NSA_GROUPS, ncp, NSA_HEAD_DIM), lambda b, i: (b, 0, 0, 0)),
            kvblk(2), kvblk(3), kvblk(4), kvblk(5),
            pl.BlockSpec((None, tq, LANES), lambda b, i: (b, i, 0)),
            pl.BlockSpec((LANES, ncp), lambda b, i: (0, 0)),
        ],
        out_specs=pl.BlockSpec((None, tq, NSA_WIDTH), lambda b, i: (b, i, 0)),
        out_shape=jax.ShapeDtypeStruct((bsz, seq, NSA_WIDTH), BF16),
        scratch_shapes=[pltpu.VMEM((NSA_GROUPS, rows, swidth), F32), pltpu.VMEM((NSA_GROUPS, rows, swidth), BF16),
                        pltpu.VMEM((NSA_GROUPS, tq, swidth), F32),
                        pltpu.VMEM((NSA_GROUPS, rows, 1), F32), pltpu.VMEM((NSA_GROUPS, rows, 1), F32),
                        pltpu.VMEM((NSA_GROUPS, rows, NSA_HEAD_DIM), F32),
                        pltpu.VMEM((NSA_GROUPS, rows, NSA_HEAD_DIM), F32)],
        compiler_params=_cp("parallel", "arbitrary"),
        name="nsa_attention",
    )(q, kvc, kv, kv, kv, kv, small, jnp.asarray(ovt, BF16))


def _mla_prep_kernel(c_ref, kr_ref, pos_ref, qnw_ref, kvnw_ref, inv_ref, cq_ref, ckv_ref, k1_ref, k2_ref,
                     cos_ref, sin_ref):
    def rms(v, w):
        return v * lax.rsqrt(jnp.mean(v * v, -1, keepdims=True) + RMS_EPS) * w

    cq_ref[...] = rms(c_ref[:, 0:MLA_Q_RANK], qnw_ref[...]).astype(BF16)
    ckv_ref[...] = rms(c_ref[:, MLA_Q_RANK:MLA_Q_RANK + MLA_KV_RANK], kvnw_ref[...]).astype(BF16)
    ang = pos_ref[...].astype(F32) * inv_ref[...]
    cos = jnp.cos(ang)
    sin = jnp.sin(ang)
    half = MLA_ROPE // 2
    t1 = kr_ref[:, 0:half]
    t2 = kr_ref[:, half:MLA_ROPE]
    k1_ref[...] = (t1 * cos - t2 * sin).astype(BF16)
    k2_ref[...] = (t1 * sin + t2 * cos).astype(BF16)
    cos_ref[...] = cos
    sin_ref[...] = sin


def _mla_prep(c, kr, pos, q_norm_w, kv_norm_w, tm=512):
    m = c.shape[0]
    half = MLA_ROPE // 2
    inv = (1.0 / (ROPE_THETA ** (jnp.arange(0, MLA_ROPE, 2, dtype=F32) / MLA_ROPE)))[None, :]
    row = lambda w: pl.BlockSpec((tm, w), lambda i: (i, 0))
    vec = lambda w: pl.BlockSpec((1, w), lambda i: (0, 0))
    return pl.pallas_call(
        _mla_prep_kernel,
        grid=(m // tm,),
        in_specs=[row(MLA_Q_RANK + MLA_KV_RANK), row(LANES), row(1), vec(MLA_Q_RANK), vec(MLA_KV_RANK), vec(half)],
        out_specs=[row(MLA_Q_RANK), row(MLA_KV_RANK), row(half), row(half), row(half), row(half)],
        out_shape=[jax.ShapeDtypeStruct((m, MLA_Q_RANK), BF16), jax.ShapeDtypeStruct((m, MLA_KV_RANK), BF16),
                   jax.ShapeDtypeStruct((m, half), BF16), jax.ShapeDtypeStruct((m, half), BF16),
                   jax.ShapeDtypeStruct((m, half), F32), jax.ShapeDtypeStruct((m, half), F32)],
        compiler_params=_cp("parallel"),
        name="mla_prep",
    )(c, kr, pos, q_norm_w[None, :], kv_norm_w[None, :], inv)


def _q_rope_kernel(t_ref, cos_ref, sin_ref, e_ref, r1_ref, r2_ref):
    hw = MLA_HEADS * MLA_ROPE // 2
    cos = _dot_f32_by_exact(cos_ref[...], e_ref[...])
    sin = _dot_f32_by_exact(sin_ref[...], e_ref[...])
    t1 = t_ref[:, 0:hw]
    t2 = t_ref[:, hw:2 * hw]
    r1_ref[...] = (t1 * cos - t2 * sin).astype(BF16)
    r2_ref[...] = (t1 * sin + t2 * cos).astype(BF16)


def _q_rope(t, cos, sin, tm=512):
    m = t.shape[0]
    half = MLA_ROPE // 2
    hw = MLA_HEADS * half
    e = (np.arange(hw)[None, :] % half == np.arange(half)[:, None])
    row = lambda w: pl.BlockSpec((tm, w), lambda i: (i, 0))
    return pl.pallas_call(
        _q_rope_kernel,
        grid=(m // tm,),
        in_specs=[row(2 * hw), row(half), row(half), pl.BlockSpec((half, hw), lambda i: (0, 0))],
        out_specs=[row(hw), row(hw)],
        out_shape=[jax.ShapeDtypeStruct((m, hw), BF16)] * 2,
        compiler_params=_cp("parallel"),
        name="mla_q_rope",
    )(t, cos, sin, jnp.asarray(e, BF16))


def _mla_attn_kernel(qn_ref, qr_ref, kn_ref, kr_ref, v_ref, o_ref, s_ref):
    t = MLA_TILE
    seq = qn_ref.shape[0]
    c = (MLA_NOPE + MLA_ROPE) ** -0.5 * float(np.log2(np.e))
    causal = _iota((t, t), 1) <= _iota((t, t), 0)
    for qi in range(seq // t):
        rows = slice(qi * t, (qi + 1) * t)
        qn = qn_ref[rows, :]
        qr = qr_ref[rows, :]
        for kt in range(qi + 1):
            keys = slice(kt * t, (kt + 1) * t)
            s = _dot_nt(qn, kn_ref[keys, :]) + _dot_nt(qr, kr_ref[keys, :])
            s_ref[:, keys] = jnp.where(causal, s, NEG_INF) if kt == qi else s
        n = (qi + 1) * t
        s = s_ref[:, 0:n]
        p = jnp.exp2((s - jnp.max(s, -1, keepdims=True)) * c)
        o = _dot(p, v_ref[0:n, :]) / jnp.sum(p, -1, keepdims=True)
        o_ref[rows, :] = o.astype(o_ref.dtype)


def _mla_attention(qn, qr, kv, kr):
    bsz, seq, _ = qn.shape
    assert seq % MLA_TILE == 0
    return pl.pallas_call(
        _mla_attn_kernel,
        grid=(bsz, MLA_HEADS),
        in_specs=[
            pl.BlockSpec((None, seq, MLA_NOPE), lambda b, h: (b, 0, h)),
            pl.BlockSpec((None, None, seq, MLA_ROPE), lambda b, h: (b, h, 0, 0)),
            pl.BlockSpec((None, seq, MLA_NOPE), lambda b, h: (b, 0, 2 * h)),
            pl.BlockSpec((None, seq, MLA_ROPE), lambda b, h: (b, 0, 0)),
            pl.BlockSpec((None, seq, MLA_V), lambda b, h: (b, 0, 2 * h + 1)),
        ],
        out_specs=pl.BlockSpec((None, seq, MLA_V), lambda b, h: (b, 0, h)),
        out_shape=jax.ShapeDtypeStruct((bsz, seq, MLA_HEADS * MLA_V), BF16),
        scratch_shapes=[pltpu.VMEM((MLA_TILE, seq), F32)],
        compiler_params=_cp("parallel", "arbitrary"),
        name="mla_attention",
    )(qn, qr, kv, kr, kv)


def _hgrn_kernel(hq_ref, hf_ref, hi_ref, hg_ref, lb_ref, nw_ref, o_ref, st):
    C = HGRN_CHUNK

    @pl.when(pl.program_id(2) == 0)
    def _():
        st[...] = jnp.zeros_like(st)

    low = _iota((C, C), 1) <= _iota((C, C), 0)
    tri = jnp.where(low, 1.0, 0.0)
    for c in range(HGRN_TILE // C):
        rs = slice(c * C, (c + 1) * C)
        for j in range(HGRN_HEADS_PER_STEP):
            cs = slice(j * HGRN_KEY, (j + 1) * HGRN_KEY)
            lb = lb_ref[:, cs]
            hf = hf_ref[rs, cs]
            v = hi_ref[rs, cs]
            q = _silu(hq_ref[rs, cs])
            f = lb + (1.0 - lb) * jax.nn.sigmoid(hf)
            k = (1.0 - lb) * jax.nn.sigmoid(-hf)
            b = _dot_exact_by_f32(tri, jnp.log(f))
            b_last = b[C - 1:C, :]
            q_t = q * jnp.exp(b)
            k_t = k * jnp.exp(-b)
            k_end = k * jnp.exp(b_last - b)
            att = jnp.where(low, _dot_nt(q_t, k_t), 0.0)
            s_t = st[j]
            o = _dot(att, v) + _dot_nt(q_t, s_t)
            st[j] = s_t * jnp.exp(b_last) + _dot(v.T, k_end)
            o = o * lax.rsqrt(jnp.mean(o * o, -1, keepdims=True) + RMS_EPS) * nw_ref[...]
            o_ref[rs, cs] = (o * _silu(hg_ref[rs, cs])).astype(o_ref.dtype)


def _hgrn_mixer(hp, lb, norm_w):
    bsz, seq, _ = hp.shape
    t = HGRN_TILE
    hps = HGRN_HEADS_PER_STEP
    w = hps * HGRN_KEY
    nh = HGRN_HEADS // hps
    part = lambda p: pl.BlockSpec((None, t, w), lambda b, h, i: (b, i, p * nh + h))
    return pl.pallas_call(
        _hgrn_kernel,
        grid=(bsz, nh, seq // t),
        in_specs=[part(0), part(1), part(2), part(3),
                  pl.BlockSpec((None, 1, w), lambda b, h, i: (h, 0, 0)),
                  pl.BlockSpec((1, HGRN_VAL), lambda b, h, i: (0, 0))],
        out_specs=pl.BlockSpec((None, t, w), lambda b, h, i: (b, i, h)),
        out_shape=jax.ShapeDtypeStruct((bsz, seq, HGRN_WIDTH), BF16),
        scratch_shapes=[pltpu.VMEM((hps, HGRN_VAL, HGRN_KEY), F32)],
        compiler_params=_cp("parallel", "parallel", "arbitrary"),
        name="hgrn2_mixer",
    )(hp, hp, hp, hp, lb.reshape(nh, 1, w), norm_w[None, :])


def _router_kernel(x_ref, wt_ref, bias_ref, esel_ref, rank_ref, wsel_ref, cnt_ref, carry):
    tm = x_ref.shape[0]
    n_grp = N_EXPERT_GROUPS
    gsz = N_EXPERTS // n_grp
    ninf = -jnp.inf

    @pl.when(pl.program_id(0) == 0)
    def _():
        carry[...] = jnp.zeros_like(carry)

    x = x_ref[...]
    w = wt_ref[...]
    xh = x.astype(BF16)
    xl = (x - xh.astype(F32)).astype(BF16)
    wh = w.astype(BF16)
    wl = (w - wh.astype(F32)).astype(BF16)
    dnt = lambda a, b: lax.dot_general(a, b, (((1,), (1,)), ((), ())), preferred_element_type=F32)
    scores = jax.nn.sigmoid(dnt(wh, xh) + dnt(wh, xl) + dnt(wl, xh))
    biased = scores + bias_ref[...]
    sub = _iota((gsz, tm), 0).astype(F32)

    def top1(blk):
        mx = jnp.max(blk, 0, keepdims=True)
        return mx, jnp.min(jnp.where(blk == mx, sub, 1e9), 0, keepdims=True)

    blocks = [biased[g * gsz:(g + 1) * gsz, :] for g in range(n_grp)]
    grp = []
    for blk in blocks:
        m1, i1 = top1(blk)
        m2, _ = top1(jnp.where(sub == i1, ninf, blk))
        grp.append(m1 + m2)
    grp = jnp.concatenate(grp, axis=0)
    gsel = jnp.zeros((n_grp, tm), F32)
    for _ in range(TOP_GROUPS):
        _, first = top1(grp)
        hit = sub == first
        gsel = jnp.where(hit, 1.0, gsel)
        grp = jnp.where(hit, ninf, grp)

    cand = [jnp.where(gsel[g:g + 1, :] > 0.5, blocks[g], ninf) for g in range(n_grp)]
    eid = [sub + float(g * gsz) for g in range(n_grp)]
    sel = [jnp.zeros((gsz, tm), F32) for _ in range(n_grp)]
    firsts = []
    for _ in range(TOP_K):
        mm = cand[0]
        for g in range(1, n_grp):
            mm = jnp.maximum(mm, cand[g])
        mx = jnp.max(mm, 0, keepdims=True)
        fm = jnp.where(cand[0] == mx, eid[0], 1e9)
        for g in range(1, n_grp):
            fm = jnp.minimum(fm, jnp.where(cand[g] == mx, eid[g], 1e9))
        first = jnp.min(fm, 0, keepdims=True)
        firsts.append(first)
        for g in range(n_grp):
            hit = eid[g] == first
            sel[g] = jnp.where(hit, 1.0, sel[g])
            cand[g] = jnp.where(hit, ninf, cand[g])

    selm = jnp.concatenate(sel, axis=0)
    before = jnp.where(_iota((tm, tm), 0) < _iota((tm, tm), 1), 1.0, 0.0).astype(BF16)
    rank = jnp.dot(selm.astype(BF16), before, preferred_element_type=F32) + carry[...]
    carry[...] = carry[...] + jnp.sum(selm, -1, keepdims=True)
    cnt_ref[...] = carry[...]

    ws, rs = [], []
    for first in firsts:
        wacc = jnp.zeros((gsz, tm), F32)
        racc = jnp.zeros((gsz, tm), F32)
        for g in range(n_grp):
            hit = eid[g] == first
            wacc = jnp.where(hit, scores[g * gsz:(g + 1) * gsz, :], wacc)
            racc = jnp.where(hit, rank[g * gsz:(g + 1) * gsz, :], racc)
        ws.append(jnp.sum(wacc, 0, keepdims=True))
        rs.append(jnp.sum(racc, 0, keepdims=True))
    wsum = ws[0]
    for wk in ws[1:]:
        wsum = wsum + wk
    esel_ref[...] = jnp.concatenate(firsts, axis=0).astype(jnp.int32)
    rank_ref[...] = jnp.concatenate(rs, axis=0).astype(jnp.int32)
    wsel_ref[...] = jnp.concatenate(ws, axis=0) / wsum * ROUTED_SCALE


def _route(x, w_router, router_bias, tm=256):
    T, d = x.shape
    E = N_EXPERTS
    assert N_EXPERT_GROUPS == E // N_EXPERT_GROUPS == 8
    kt = pl.BlockSpec((TOP_K, tm), lambda i: (0, i))
    return pl.pallas_call(
        _router_kernel,
        grid=(T // tm,),
        in_specs=[pl.BlockSpec((tm, d), lambda i: (i, 0)), pl.BlockSpec((E, d), lambda i: (0, 0)),
                  pl.BlockSpec((E, 1), lambda i: (0, 0))],
        out_specs=[kt, kt, kt, pl.BlockSpec((E, 1), lambda i: (0, 0))],
        out_shape=[jax.ShapeDtypeStruct((TOP_K, T), jnp.int32), jax.ShapeDtypeStruct((TOP_K, T), jnp.int32),
                   jax.ShapeDtypeStruct((TOP_K, T), F32), jax.ShapeDtypeStruct((E, 1), F32)],
        scratch_shapes=[pltpu.VMEM((E, 1), F32)],
        compiler_params=_cp("arbitrary"),
        name="moe_router",
    )(x, w_router.T, router_bias.astype(F32)[:, None])


def _expert_kernel(be_ref, nv_ref, x_ref, wg_ref, wu_ref, wd_ref, o_ref, wg_b, wu_b, wd_b):
    i = pl.program_id(0)

    @pl.when((i == 0) | (be_ref[i] != be_ref[jnp.maximum(i - 1, 0)]))
    def _():
        wg_b[...] = wg_ref[...].astype(BF16)
        wu_b[...] = wu_ref[...].astype(BF16)
        wd_b[...] = wd_ref[...].astype(BF16)

    @pl.when(i < nv_ref[0])
    def _():
        x = x_ref[...]
        h = _silu(jnp.dot(x, wg_b[...], preferred_element_type=F32)) * jnp.dot(
            x, wu_b[...], preferred_element_type=F32)
        o_ref[...] = _pack_bf16_pairs(jnp.dot(h.astype(BF16), wd_b[...], preferred_element_type=F32))

    @pl.when(i >= nv_ref[0])
    def _():
        o_ref[...] = jnp.zeros_like(o_ref)


def _pack_bf16_pairs(y):
    half = y.shape[1] // 2
    bits = lambda v: lax.bitcast_convert_type(v.astype(BF16).astype(F32), jnp.uint32)
    return bits(y[:, :half]) | (bits(y[:, half:]) >> 16)


def _unpack_bf16_pairs(u):
    hi = lax.bitcast_convert_type(u & jnp.uint32(0xFFFF0000), F32)
    lo = lax.bitcast_convert_type(u << 16, F32)
    return hi, lo


def _expert_ffn(x_rows, blk_e, n_valid, w_gate, w_up, w_down, layer, mb):
    p, d = x_rows.shape
    ff = w_gate.shape[3]
    xmap = lambda i, be, nv: (jnp.minimum(i, nv[0] - 1), 0)
    wmap = lambda i, be, nv: (layer, be[i], 0, 0)
    grid_spec = pltpu.PrefetchScalarGridSpec(
        num_scalar_prefetch=2,
        grid=(p // mb,),
        in_specs=[
            pl.BlockSpec((mb, d), xmap),
            pl.BlockSpec((None, None, d, ff), wmap), pl.BlockSpec((None, None, d, ff), wmap),
            pl.BlockSpec((None, None, ff, d), wmap),
        ],
        out_specs=pl.BlockSpec((mb, d // 2), lambda i, be, nv: (i, 0)),
        scratch_shapes=[pltpu.VMEM((d, ff), BF16), pltpu.VMEM((d, ff), BF16), pltpu.VMEM((ff, d), BF16)],
    )
    return pl.pallas_call(
        _expert_kernel,
        grid_spec=grid_spec,
        out_shape=jax.ShapeDtypeStruct((p, d // 2), jnp.uint32),
        compiler_params=pltpu.CompilerParams(dimension_semantics=("arbitrary",), vmem_limit_bytes=EXPERT_VMEM_LIMIT),
        name="moe_expert_ffn",
    )(blk_e, n_valid, x_rows, w_gate, w_up, w_down)


def _moe_ffn(x, x_bf, layer, w_router, router_bias, w_gate, w_up, w_down, ws_gate, ws_up, ws_down):
    T, d = x.shape
    E = N_EXPERTS
    M = MOE_ROW_BLOCK
    esel, rank, wsel, cnt = _route(x, w_router, router_bias)
    counts = cnt[:, 0].astype(jnp.int32)
    pcounts = (counts + M - 1) // M * M
    pends = jnp.cumsum(pcounts)
    pstarts = pends - pcounts
    NB = -(-(T * TOP_K) // M) + E
    blk_e = jnp.minimum(jnp.sum(pends[None, :] <= (jnp.arange(NB, dtype=jnp.int32) * M)[:, None], axis=1),
                        E - 1).astype(jnp.int32)
    n_valid = (pends[-1] // M).astype(jnp.int32).reshape(1)
    slot = rank + jnp.sum(jnp.where(esel[:, :, None] == jnp.arange(E, dtype=jnp.int32), pstarts, 0), axis=-1)
    slot = slot.reshape(TOP_K * T)
    row_tok = jnp.zeros((NB * M,), jnp.int32).at[slot].set(
        jnp.tile(jnp.arange(T, dtype=jnp.int32), TOP_K), unique_indices=True)
    y_rows = _expert_ffn(x_bf[row_tok], blk_e, n_valid, w_gate, w_up, w_down, layer, M)
    y_k = y_rows[slot].reshape(TOP_K, T, d // 2)

    depth = ws_gate.shape[0]
    ms = SHARED_ROW_BLOCK
    shared = _expert_ffn(x_bf, jnp.zeros((T // ms,), jnp.int32), jnp.full((1,), T // ms, jnp.int32),
                         ws_gate.reshape(depth, 1, d, -1), ws_up.reshape(depth, 1, d, -1),
                         ws_down.reshape(depth, 1, -1, d), layer, ms)
    return shared, y_k, wsel.T


def _mixer_ab(x_bf, bsz, seq, w_in, conv_w, conv_b, dt_bias, a_log, d_skip, norm_w, cmp_pos, cmp_w, w_out):
    T = bsz * seq
    o_zx = SSM_INNER + SSM_CONV_DIM
    o_dt = o_zx + SSM_HEADS
    o_q = o_dt + NSA_WIDTH
    o_kv = o_q + 6 * NSA_KV
    w_bf = w_in.astype(BF16)
    zx = _matmul(x_bf, w_bf[:, :o_zx], F32)
    w_small = _pad_cols(jnp.concatenate([w_bf[:, o_zx:o_dt], w_bf[:, o_kv:]], axis=1), LANES)
    small = _matmul(x_bf, w_small, F32)
    q = _matmul(x_bf, w_bf[:, o_dt:o_q], BF16)
    kv = _matmul(x_bf, w_bf[:, o_q:o_kv], BF16)

    zx3 = zx.reshape(bsz, seq, o_zx)
    small3 = small.reshape(bsz, seq, LANES)
    dt_t = jnp.swapaxes(small3[:, :, :SSM_HEADS], 1, 2)
    y_a = _ssd_mixer(zx3, small3, dt_t, conv_w, conv_b, dt_bias, a_log, d_skip, norm_w)

    kv3 = kv.reshape(bsz, seq, 6 * NSA_KV)
    ncp = seq // NSA_CMP_STRIDE
    strips = kv3[:, :, :2 * NSA_KV].reshape(bsz, ncp, NSA_CMP_STRIDE, 2 * NSA_GROUPS, NSA_HEAD_DIM)
    strips = strips.transpose(0, 3, 1, 2, 4).reshape(bsz, 2 * NSA_GROUPS, ncp, NSA_CMP_STRIDE * NSA_HEAD_DIM)
    kvc = _nsa_compress(strips, cmp_w, cmp_pos)
    y_b = _nsa_mixer(q.reshape(bsz, seq, NSA_WIDTH), kv3, small3, kvc)

    return _matmul_cat(y_a.reshape(T, SSM_INNER), y_b.reshape(T, NSA_WIDTH), w_out.astype(BF16), F32)


def _mixer_cd(x_bf, bsz, seq, positions, w_in, q_norm_w, w_uq, kv_norm_w, w_ukv, lb, hgrn_norm_w, w_out):
    T = bsz * seq
    o_c = MLA_Q_RANK + MLA_KV_RANK
    o_kr = o_c + MLA_ROPE
    half = MLA_ROPE // 2
    w_bf = w_in.astype(BF16)
    c = _matmul(x_bf, w_bf[:, :o_c], F32)
    kr = _matmul(x_bf, _pad_cols(w_bf[:, o_c:o_kr], LANES), F32)
    hp = _matmul(x_bf, w_bf[:, o_kr:], F32)

    cqn, ckvn, k1, k2, cos, sin = _mla_prep(c, kr, positions.reshape(T, 1), q_norm_w, kv_norm_w)
    wq = w_uq.astype(BF16).reshape(MLA_Q_RANK, MLA_HEADS, MLA_NOPE + MLA_ROPE)
    w_nope = wq[:, :, :MLA_NOPE].reshape(MLA_Q_RANK, MLA_HEADS * MLA_NOPE)
    w_rope = jnp.concatenate([wq[:, :, MLA_NOPE:MLA_NOPE + half].reshape(MLA_Q_RANK, MLA_HEADS * half),
                              wq[:, :, MLA_NOPE + half:].reshape(MLA_Q_RANK, MLA_HEADS * half)], axis=1)
    qn = _matmul(cqn, w_nope, BF16)
    r1, r2 = _q_rope(_matmul(cqn, w_rope, F32), cos, sin)
    qr = jnp.concatenate([r1.reshape(bsz, seq, MLA_HEADS, half), r2.reshape(bsz, seq, MLA_HEADS, half)], axis=-1)
    qr = qr.transpose(0, 2, 1, 3)
    kvu = _matmul(ckvn, w_ukv.astype(BF16), BF16)
    y_c = _mla_attention(qn.reshape(bsz, seq, -1), qr, kvu.reshape(bsz, seq, -1),
                         jnp.concatenate([k1, k2], axis=1).reshape(bsz, seq, MLA_ROPE))
    y_d = _hgrn_mixer(hp.reshape(bsz, seq, -1), lb, hgrn_norm_w)
    return _matmul_cat(y_c.reshape(T, -1), y_d.reshape(T, -1), w_out.astype(BF16), F32)


def kernel(x, positions, ab_w_in, ssm_conv_w, ssm_conv_b, ssm_dt_bias, ssm_a_log, ssm_d, ssm_norm_w, nsa_cmp_pos, nsa_cmp_w, ab_w_out, cd_w_in, mla_q_norm_w, mla_w_uq, mla_kv_norm_w, mla_w_ukv, hgrn_lb_logits, hgrn_norm_w, cd_w_out, ln_g, ln_b, moe_w_router, moe_router_bias, moe_w_gate, moe_w_up, moe_w_down, moe_shared_w_gate, moe_shared_w_up, moe_shared_w_down):
    bsz, seq, d = x.shape
    T = bsz * seq
    lb_all = jnp.cumsum(jax.nn.softmax(hgrn_lb_logits.astype(F32), axis=0), axis=0)
    lb_all = lb_all - lb_all[0]
    xf = x.reshape(T, d)
    x_bf = xf.astype(BF16)
    for l in range(DEPTH):
        i = l // 2
        if l % 2 == 0:
            mix = _mixer_ab(x_bf, bsz, seq, ab_w_in[i], ssm_conv_w[i], ssm_conv_b[i], ssm_dt_bias[i], ssm_a_log[i],
                            ssm_d[i], ssm_norm_w[i], nsa_cmp_pos[i], nsa_cmp_w[i], ab_w_out[i])
        else:
            mix = _mixer_cd(x_bf, bsz, seq, positions, cd_w_in[i], mla_q_norm_w[i], mla_w_uq[i], mla_kv_norm_w[i],
                            mla_w_ukv[i], lb_all[l], hgrn_norm_w[i], cd_w_out[i])
        xf, x_bf = _deepnorm_ln(xf, [mix], ln_g[l, 0], ln_b[l, 0])
        shared, y_k, w_k = _moe_ffn(xf, x_bf, l, moe_w_router[l], moe_router_bias[l], moe_w_gate, moe_w_up,
                                    moe_w_down, moe_shared_w_gate, moe_shared_w_up, moe_shared_w_down)
        xf, x_bf = _deepnorm_ln_combine(xf, shared, y_k, w_k, ln_g[l, 1], ln_b[l, 1])
    return xf.reshape(bsz, seq, d)
```

```python
import functools

import numpy as np
import jax
import jax.numpy as jnp
from jax import lax
from jax.experimental import pallas as pl
from jax.experimental.pallas import tpu as pltpu

F32 = jnp.float32
BF16 = jnp.bfloat16

D_MODEL = 2048
DEPTH = 2
DN_ALPHA = float((2 * DEPTH) ** 0.25)
LN_EPS = 1e-5
RMS_EPS = 1e-6
NEG_INF = -1e30

SSM_HEAD_DIM = 64
SSM_INNER = D_MODEL
SSM_HEADS = SSM_INNER // SSM_HEAD_DIM
SSM_STATE = 128
SSM_GROUPS = 4
SSM_CONV = 4
SSM_CHUNK = 128
SSM_BC = SSM_GROUPS * SSM_STATE
SSM_CONV_DIM = SSM_INNER + 2 * SSM_BC
SSM_GROUP_W = SSM_INNER // SSM_GROUPS

NSA_HEADS = 16
NSA_GROUPS = 2
NSA_HG = NSA_HEADS // NSA_GROUPS
NSA_HEAD_DIM = D_MODEL // NSA_HEADS
NSA_WIDTH = NSA_HEADS * NSA_HEAD_DIM
NSA_KV = NSA_GROUPS * NSA_HEAD_DIM
NSA_CMP_BLOCK = 32
NSA_CMP_STRIDE = 16
NSA_SEL_BLOCK = 64
NSA_TOP_N = 8
NSA_WINDOW = 512
NSA_FORCED = 1e6
NSA_Q_TILE = 128
NSA_SEL_K_TILE = 512

MLA_HEADS = 16
MLA_Q_RANK = 512
MLA_KV_RANK = 512
MLA_NOPE = 128
MLA_ROPE = 64
MLA_V = 128
ROPE_THETA = 10000.0
MLA_TILE = 512

HGRN_HEADS = 16
HGRN_KEY = 128
HGRN_VAL = 128
HGRN_CHUNK = 64
HGRN_FDIM = HGRN_HEADS * HGRN_KEY
HGRN_WIDTH = HGRN_HEADS * HGRN_VAL
HGRN_TILE = 512
HGRN_HEADS_PER_STEP = 2

N_EXPERTS = 64
TOP_K = 8
N_EXPERT_GROUPS = 8
TOP_GROUPS = 4
EXPERT_FF = 512
ROUTED_SCALE = 2.5
MOE_ROW_BLOCK = 512
SHARED_ROW_BLOCK = 512

LANES = 128
VMEM_LIMIT = 48 * 1024 * 1024
EXPERT_VMEM_LIMIT = 56 * 1024 * 1024


def _cp(*sem):
    return pltpu.CompilerParams(dimension_semantics=sem, vmem_limit_bytes=VMEM_LIMIT)


def _dot(a, b):
    return jnp.dot(a.astype(BF16), b.astype(BF16), preferred_element_type=F32)


def _dot_nt(a, b):
    return lax.dot_general(a.astype(BF16), b.astype(BF16), (((1,), (1,)), ((), ())),
                           preferred_element_type=F32)


def _split3(a):
    a1 = a.astype(BF16)
    r1 = a - a1.astype(F32)
    a2 = r1.astype(BF16)
    r2 = r1 - a2.astype(F32)
    return a1, a2, r2.astype(BF16)


def _dot_f32_by_exact(a, t):
    t = t.astype(BF16)
    return sum(jnp.dot(p, t, preferred_element_type=F32) for p in _split3(a))


def _dot_exact_by_f32(t, a):
    t = t.astype(BF16)
    return sum(jnp.dot(t, p, preferred_element_type=F32) for p in _split3(a))


def _silu(x):
    return x * jax.nn.sigmoid(x)


def _softplus(x):
    return jnp.maximum(x, 0.0) + jnp.log(1.0 + jnp.exp(-jnp.abs(x)))


def _iota(shape, dim):
    return lax.broadcasted_iota(jnp.int32, shape, dim)


def _mm_kernel(x_ref, w_ref, o_ref):
    o_ref[...] = jnp.dot(x_ref[...], w_ref[...], preferred_element_type=F32).astype(o_ref.dtype)


def _mm2_kernel(xa_ref, xb_ref, w_ref, o_ref):
    ka = xa_ref.shape[1]
    acc = jnp.dot(xa_ref[...], w_ref[0:ka, :], preferred_element_type=F32)
    acc = acc + jnp.dot(xb_ref[...], w_ref[ka:, :], preferred_element_type=F32)
    o_ref[...] = acc.astype(o_ref.dtype)


def _pick_tile(n, pref):
    if n <= pref:
        return n
    t = pref - pref % LANES
    while n % t:
        t -= LANES
    return t


def _matmul(x, w, out_dtype, tm=2048, tn=512):
    m, k = x.shape
    n = w.shape[1]
    tm = _pick_tile(m, tm)
    tn = _pick_tile(n, tn)
    return pl.pallas_call(
        _mm_kernel,
        grid=(m // tm, n // tn),
        in_specs=[pl.BlockSpec((tm, k), lambda i, j: (i, 0)), pl.BlockSpec((k, tn), lambda i, j: (0, j))],
        out_specs=pl.BlockSpec((tm, tn), lambda i, j: (i, j)),
        out_shape=jax.ShapeDtypeStruct((m, n), out_dtype),
        compiler_params=_cp("parallel", "arbitrary"),
        name="matmul",
    )(x, w)


def _matmul_cat(xa, xb, w, out_dtype, tm=1024, tn=1024):
    m, ka = xa.shape
    kb = xb.shape[1]
    n = w.shape[1]
    tm = _pick_tile(m, tm)
    tn = _pick_tile(n, tn)
    return pl.pallas_call(
        _mm2_kernel,
        grid=(m // tm, n // tn),
        in_specs=[pl.BlockSpec((tm, ka), lambda i, j: (i, 0)), pl.BlockSpec((tm, kb), lambda i, j: (i, 0)),
                  pl.BlockSpec((ka + kb, tn), lambda i, j: (0, j))],
        out_specs=pl.BlockSpec((tm, tn), lambda i, j: (i, j)),
        out_shape=jax.ShapeDtypeStruct((m, n), out_dtype),
        compiler_params=_cp("parallel", "arbitrary"),
        name="matmul_cat",
    )(xa, xb, w)


def _pad_cols(w, mult):
    n = w.shape[1]
    pad = (-n) % mult
    return jnp.pad(w, ((0, 0), (0, pad))) if pad else w


def _ln_combine_kernel(x_ref, sh_ref, slot_hbm, y_hbm, w_ref, g_ref, b_ref, o_ref, obf_ref,
                       idx, rows, idx_sem, row_sem):
    i = pl.program_id(0)
    n = pl.num_programs(0)
    tm = x_ref.shape[0]

    def idx_copy(step, s):
        return pltpu.make_async_copy(slot_hbm.at[pl.ds(step, 1), :], idx.at[pl.ds(s, 1), :], idx_sem.at[s])

    def row_copy(src_row, s, k, t):
        return pltpu.make_async_copy(y_hbm.at[pl.ds(src_row, 1), :], rows.at[s, k, pl.ds(t, 1), :], row_sem.at[s])

    def start_rows(s):
        def body(t, carry):
            for k in range(TOP_K):
                row_copy(idx[s, k * tm + t], s, k, t).start()
            return carry
        lax.fori_loop(0, tm, body, 0, unroll=8)

    def wait_rows(s):
        def body(t, carry):
            for k in range(TOP_K):
                row_copy(0, s, k, t).wait()
            return carry
        lax.fori_loop(0, tm, body, 0, unroll=8)

    cur = i % 2
    nxt = 1 - cur

    @pl.when(i == 0)
    def _():
        idx_copy(0, 0).start()
        idx_copy(0, 0).wait()
        start_rows(0)

        @pl.when(n > 1)
        def _():
            idx_copy(1, 1).start()

    @pl.when(i + 1 < n)
    def _():
        idx_copy(i + 1, nxt).wait()
        start_rows(nxt)

    @pl.when(i + 2 < n)
    def _():
        idx_copy(i + 2, cur).start()

    hi, lo = _unpack_bf16_pairs(sh_ref[...])
    w = w_ref[...]
    wait_rows(cur)
    for k in range(TOP_K):
        yh, yl = _unpack_bf16_pairs(rows[cur, k])
        hi = hi + w[:, k:k + 1] * yh
        lo = lo + w[:, k:k + 1] * yl
    v = DN_ALPHA * x_ref[...] + jnp.concatenate([hi, lo], axis=1)
    mu = jnp.mean(v, -1, keepdims=True)
    d = v - mu
    var = jnp.mean(d * d, -1, keepdims=True)
    y = d * lax.rsqrt(var + LN_EPS) * g_ref[...] + b_ref[...]
    o_ref[...] = y
    obf_ref[...] = y.astype(BF16)


def _deepnorm_ln_combine(x, shared, y_rows, slot, w_k, g, b, tm=128):
    m, d = x.shape
    nt = m // tm
    slot_tiles = slot.reshape(TOP_K, nt, tm).transpose(1, 0, 2).reshape(nt, TOP_K * tm)
    row = pl.BlockSpec((tm, d), lambda i: (i, 0))
    vec = pl.BlockSpec((1, d), lambda i: (0, 0))
    return pl.pallas_call(
        _ln_combine_kernel,
        grid=(nt,),
        in_specs=[row, pl.BlockSpec((tm, d // 2), lambda i: (i, 0)),
                  pl.BlockSpec(memory_space=pl.ANY), pl.BlockSpec(memory_space=pl.ANY),
                  pl.BlockSpec((tm, TOP_K), lambda i: (i, 0)), vec, vec],
        out_specs=[row, row],
        out_shape=[jax.ShapeDtypeStruct((m, d), F32), jax.ShapeDtypeStruct((m, d), BF16)],
        scratch_shapes=[pltpu.SMEM((2, TOP_K * tm), jnp.int32), pltpu.VMEM((2, TOP_K, tm, d // 2), jnp.uint32),
                        pltpu.SemaphoreType.DMA((2,)), pltpu.SemaphoreType.DMA((2,))],
        compiler_params=_cp("arbitrary"),
        name="deepnorm_ln_combine",
    )(x, shared, slot_tiles, y_rows, w_k, g.reshape(1, d), b.reshape(1, d))


def _ln_kernel(*refs, n_add):
    x_ref = refs[0]
    add_refs = refs[1:1 + n_add]
    g_ref, b_ref, o_ref, obf_ref = refs[1 + n_add:]
    v = DN_ALPHA * x_ref[...]
    for r in add_refs:
        v = v + r[...].astype(F32)
    mu = jnp.mean(v, -1, keepdims=True)
    d = v - mu
    var = jnp.mean(d * d, -1, keepdims=True)
    y = d * lax.rsqrt(var + LN_EPS) * g_ref[...] + b_ref[...]
    o_ref[...] = y
    obf_ref[...] = y.astype(BF16)


def _deepnorm_ln(x, adds, g, b, tm=256):
    m, d = x.shape
    row = pl.BlockSpec((tm, d), lambda i: (i, 0))
    vec = pl.BlockSpec((1, d), lambda i: (0, 0))
    return pl.pallas_call(
        functools.partial(_ln_kernel, n_add=len(adds)),
        grid=(m // tm,),
        in_specs=[row] * (1 + len(adds)) + [vec, vec],
        out_specs=[row, row],
        out_shape=[jax.ShapeDtypeStruct((m, d), F32), jax.ShapeDtypeStruct((m, d), BF16)],
        compiler_params=_cp("parallel"),
        name="deepnorm_ln",
    )(x, *adds, g.reshape(1, d), b.reshape(1, d))


def _ssd_kernel(z_ref, xs_ref, bc_ref, dt_ref, dtt_ref, cwx_ref, cbx_ref, cwb_ref, cbb_ref, dtb_ref, dtbt_ref,
                alog_ref, alogt_ref, dexp_ref, nw_ref, e_ref, o_ref, extx, extb, st, y_acc):
    L = SSM_CHUNK
    halo = 8

    @pl.when(pl.program_id(1) == 0)
    def _():
        extx[0:halo, :] = jnp.zeros((halo, SSM_INNER), F32)
        extb[0:halo, :] = jnp.zeros((halo, 2 * SSM_BC), F32)
        st[...] = jnp.zeros_like(st)

    extx[halo:halo + L, :] = xs_ref[...]
    extb[halo:halo + L, :] = bc_ref[...]

    def conv(ext, w_ref, b_ref):
        acc = b_ref[...]
        for k in range(SSM_CONV):
            acc = acc + ext[pl.ds(halo - (SSM_CONV - 1) + k, L), :] * w_ref[k:k + 1, :]
        return acc

    xs = _silu(conv(extx, cwx_ref, cbx_ref))
    bc = _silu(conv(extb, cwb_ref, cbb_ref))
    extx[0:halo, :] = xs_ref[L - halo:L, :]
    extb[0:halo, :] = bc_ref[L - halo:L, :]

    li = _iota((L, L), 0)
    si = _iota((L, L), 1)
    low = si <= li
    dt = _softplus(dt_ref[:, 0:SSM_HEADS] + dtb_ref[...])
    adt = dt * (-jnp.exp(alog_ref[...]))
    a_cs = _dot_exact_by_f32(jnp.where(low, 1.0, 0.0), adt)
    dtt = _softplus(dtt_ref[...] + dtbt_ref[...])
    adtt = dtt * (-jnp.exp(alogt_ref[...]))
    a_cst = _dot_f32_by_exact(adtt, jnp.where(li <= si, 1.0, 0.0))

    e = e_ref[...]
    dt_e = _dot_f32_by_exact(dt, e)
    acs_e = _dot_f32_by_exact(a_cs, e)
    a_last = acs_e[L - 1:L, :]
    x_dt = xs * dt_e
    xw = (x_dt * jnp.exp(a_last - acs_e)).astype(BF16)
    ea = jnp.exp(acs_e)
    chunk_decay = jnp.exp(a_last)
    x_dt_b = x_dt.astype(BF16)
    lane = _iota((L, 2 * SSM_HEAD_DIM), 1)

    for g in range(SSM_GROUPS):
        gc = slice(g * SSM_GROUP_W, (g + 1) * SSM_GROUP_W)
        bm = bc[:, g * SSM_STATE:(g + 1) * SSM_STATE]
        cm = bc[:, SSM_BC + g * SSM_STATE:SSM_BC + (g + 1) * SSM_STATE]
        cb = _dot_nt(cm, bm)
        st_g = st[:, gc]
        y_acc[:, gc] = _dot(cm, st_g) * ea[:, gc]
        for pr in range(SSM_GROUP_W // (2 * SSM_HEAD_DIM)):
            h0 = (g * SSM_GROUP_W) // SSM_HEAD_DIM + 2 * pr
            cols = slice(h0 * SSM_HEAD_DIM, (h0 + 2) * SSM_HEAD_DIM)
            xr = x_dt_b[:, cols]
            res = []
            for h in (h0, h0 + 1):
                seg = a_cs[:, h:h + 1] - a_cst[h:h + 1, :]
                res.append(_dot(cb * jnp.where(low, jnp.exp(seg), 0.0), xr))
            y_acc[:, cols] += jnp.where(lane < SSM_HEAD_DIM, res[0], res[1])
        st[:, gc] = st_g * chunk_decay[:, gc] + _dot(bm.T, xw[:, gc])

    y = y_acc[...] + xs * dexp_ref[...]
    y = y * _silu(z_ref[...])
    var = jnp.mean(y * y, -1, keepdims=True)
    o_ref[...] = (y * lax.rsqrt(var + RMS_EPS) * nw_ref[...]).astype(o_ref.dtype)


def _ssd_mixer(zx, small, dt_t, conv_w, conv_b, dt_bias, a_log, d_skip, norm_w):
    bsz, seq, _ = zx.shape
    L = SSM_CHUNK
    nblk = SSM_INNER // (2 * SSM_BC)
    e = (np.arange(SSM_INNER)[None, :] // SSM_HEAD_DIM == np.arange(SSM_HEADS)[:, None])
    full = lambda shape: pl.BlockSpec(shape, lambda b, c: (0,) * len(shape))
    return pl.pallas_call(
        _ssd_kernel,
        grid=(bsz, seq // L),
        in_specs=[
            pl.BlockSpec((None, L, SSM_INNER), lambda b, c: (b, c, 0)),
            pl.BlockSpec((None, L, SSM_INNER), lambda b, c: (b, c, 1)),
            pl.BlockSpec((None, L, 2 * SSM_BC), lambda b, c: (b, c, 2 * nblk)),
            pl.BlockSpec((None, L, LANES), lambda b, c: (b, c, 0)),
            pl.BlockSpec((None, SSM_HEADS, L), lambda b, c: (b, 0, c)),
            full((SSM_CONV, SSM_INNER)), full((1, SSM_INNER)),
            full((SSM_CONV, 2 * SSM_BC)), full((1, 2 * SSM_BC)),
            full((1, SSM_HEADS)), full((SSM_HEADS, 1)), full((1, SSM_HEADS)), full((SSM_HEADS, 1)),
            full((1, SSM_INNER)), full((1, SSM_INNER)), full((SSM_HEADS, SSM_INNER)),
        ],
        out_specs=pl.BlockSpec((None, L, SSM_INNER), lambda b, c: (b, c, 0)),
        out_shape=jax.ShapeDtypeStruct((bsz, seq, SSM_INNER), BF16),
        scratch_shapes=[
            pltpu.VMEM((L + 8, SSM_INNER), F32), pltpu.VMEM((L + 8, 2 * SSM_BC), F32),
            pltpu.VMEM((SSM_STATE, SSM_INNER), F32), pltpu.VMEM((L, SSM_INNER), F32),
        ],
        compiler_params=_cp("arbitrary", "arbitrary"),
        name="ssd_mixer",
    )(zx, zx, zx, small, dt_t,
      conv_w[:, :SSM_INNER], conv_b[None, :SSM_INNER], conv_w[:, SSM_INNER:], conv_b[None, SSM_INNER:],
      dt_bias[None, :], dt_bias[:, None], a_log[None, :], a_log[:, None],
      jnp.repeat(d_skip, SSM_HEAD_DIM)[None, :], norm_w[None, :], jnp.asarray(e, BF16))


def _nsa_compress_kernel(r_ref, w_ref, pos_ref, o_ref):
    ncp = r_ref.shape[0]
    half = NSA_CMP_STRIDE * NSA_HEAD_DIM
    r = r_ref[...].astype(F32)
    top = _dot(r + pos_ref[0:1, :], w_ref[0:half, :])
    bot = _dot(r + pos_ref[1:2, :], w_ref[half:2 * half, :])
    kc = top + pltpu.roll(bot, ncp - 1, axis=0)
    o_ref[...] = jnp.where(_iota(kc.shape, 0) < ncp - 1, kc, 0.0).astype(o_ref.dtype)


def _nsa_compress(strips, cmp_w, cmp_pos):
    bsz, _, ncp, width = strips.shape
    return pl.pallas_call(
        _nsa_compress_kernel,
        grid=(bsz, 2 * NSA_GROUPS),
        in_specs=[
            pl.BlockSpec((None, None, ncp, width), lambda b, j: (b, j, 0, 0)),
            pl.BlockSpec((None, 2 * width, NSA_HEAD_DIM), lambda b, j: (j // NSA_GROUPS, 0, 0)),
            pl.BlockSpec((None, 2, width), lambda b, j: (j // NSA_GROUPS, 0, 0)),
        ],
        out_specs=pl.BlockSpec((None, None, ncp, NSA_HEAD_DIM), lambda b, j: (b, j, 0, 0)),
        out_shape=jax.ShapeDtypeStruct((bsz, 2 * NSA_GROUPS, ncp, NSA_HEAD_DIM), BF16),
        compiler_params=_cp("parallel", "arbitrary"),
        name="nsa_compress",
    )(strips, cmp_w.astype(BF16), cmp_pos.reshape(2, 2, width))


def _nsa_kernel(q_ref, kvc_ref, ksel_ref, vsel_ref, kwin_ref, vwin_ref, gate_ref, ovt_ref, o_ref,
                s_ref, p_ref, bias_ref, m_ref, l_ref, acc_ref, ocmp_ref, *, seq):
    tq = NSA_Q_TILE
    tk = NSA_SEL_K_TILE
    dh = NSA_HEAD_DIM
    ncp = seq // NSA_CMP_STRIDE
    n_cmp = ncp - 1
    n_sel = seq // NSA_SEL_BLOCK
    top_n = min(NSA_TOP_N, n_sel)
    wlen = NSA_WINDOW + tq
    c = dh ** -0.5 * float(np.log2(np.e))
    q0 = pl.program_id(1) * tq
    head_rows = [slice(h * tq, (h + 1) * tq) for h in range(NSA_HG)]

    gates = jax.nn.sigmoid(gate_ref[...])
    has_cmp = q0 + _iota((tq, 1), 0) >= NSA_CMP_BLOCK - 1
    ci = _iota((tq, ncp), 1)
    cmp_ok = (ci * NSA_CMP_STRIDE + (NSA_CMP_BLOCK - 1) <= q0 + _iota((tq, ncp), 0)) & (ci < n_cmp)
    cmp_bias = jnp.where(cmp_ok, 0.0, NEG_INF)

    groups = range(NSA_GROUPS)
    gcs = [slice(g * dh, (g + 1) * dh) for g in groups]
    qss = [jnp.concatenate([q_ref[:, (g * NSA_HG + h) * dh:(g * NSA_HG + h + 1) * dh] for h in range(NSA_HG)],
                           axis=0) for g in groups]
    selbs = []
    for g in groups:
        s_ref[g, :, 0:ncp] = _dot_nt(qss[g], kvc_ref[g])
        psum = jnp.zeros((tq, ncp), F32)
        for rs in head_rows:
            sc = s_ref[g, rs, 0:ncp] + cmp_bias
            pe = jnp.exp2((sc - jnp.max(sc, -1, keepdims=True)) * c)
            p = jnp.where(has_cmp, pe / jnp.sum(pe, -1, keepdims=True), 0.0)
            psum = psum + p
            p_ref[g, rs, 0:ncp] = p.astype(BF16)
        ocmp_ref[g] = _dot(p_ref[g, :, 0:ncp], kvc_ref[NSA_GROUPS + g])

        ovt = ovt_ref[...]
        imp = sum(lax.dot_general(ovt, part, (((1,), (1,)), ((), ())), preferred_element_type=F32)
                  for part in _split3(psum))
        j = _iota((LANES, tq), 0)
        tok = q0 + _iota((LANES, tq), 1)
        cur = tok >> 6
        forced = (j == 0) | (j == cur) | (j == cur - 1)
        imp = jnp.where(forced, NSA_FORCED, jnp.where(j * NSA_SEL_BLOCK <= tok, imp, -1.0))
        imp = jnp.where(j < n_sel, imp, -2.0)
        jf = j.astype(F32)
        sel = jnp.zeros((LANES, tq), F32)
        for _ in range(top_n):
            mx = jnp.max(imp, 0, keepdims=True)
            first = jnp.min(jnp.where(imp == mx, jf, 1e9), 0, keepdims=True)
            hit = jf == first
            sel = jnp.where(hit, 1.0, sel)
            imp = jnp.where(hit, -3.0, imp)
        selbs.append(sel.T.astype(BF16))

    m_ref[...] = jnp.full(m_ref.shape, NEG_INF, F32)
    l_ref[...] = jnp.zeros(l_ref.shape, F32)
    acc_ref[...] = jnp.zeros(acc_ref.shape, F32)

    def sel_step(kt, carry):
        k0 = pl.multiple_of(kt * tk, tk)
        blk_of_key = (k0 + _iota((LANES, tk), 1)) >> 6
        expand = jnp.where(blk_of_key == _iota((LANES, tk), 0), 1.0, 0.0).astype(BF16)
        causal = k0 + _iota((tq, tk), 1) <= q0 + _iota((tq, tk), 0)
        for g in groups:
            s_ref[g, :, 0:tk] = _dot_nt(qss[g], ksel_ref[pl.ds(k0, tk), gcs[g]])
            picked = jnp.dot(selbs[g], expand, preferred_element_type=F32)
            bias_ref[g, :, 0:tk] = jnp.where((picked > 0.5) & causal, 0.0, NEG_INF)
            for rs in head_rows:
                sc = s_ref[g, rs, 0:tk] + bias_ref[g, :, 0:tk]
                m_old = m_ref[g, rs, :]
                m_new = jnp.maximum(m_old, jnp.max(sc, -1, keepdims=True))
                alpha = jnp.exp2((m_old - m_new) * c)
                pe = jnp.exp2((sc - m_new) * c)
                l_ref[g, rs, :] = alpha * l_ref[g, rs, :] + jnp.sum(pe, -1, keepdims=True)
                acc_ref[g, rs, :] = alpha * acc_ref[g, rs, :]
                m_ref[g, rs, :] = m_new
                p_ref[g, rs, 0:tk] = pe.astype(BF16)
            acc_ref[g] += _dot(p_ref[g, :, 0:tk], vsel_ref[pl.ds(k0, tk), gcs[g]])
        return carry

    lax.fori_loop(0, (q0 + tq + tk - 1) // tk, sel_step, 0)

    ws = pl.multiple_of(jnp.maximum(q0 - NSA_WINDOW, 0), tq)
    kpos = ws + _iota((tq, wlen), 1)
    tpos = q0 + _iota((tq, wlen), 0)
    win_bias = jnp.where((kpos <= tpos) & (kpos > tpos - NSA_WINDOW), 0.0, NEG_INF)
    for g in groups:
        o_sel = acc_ref[g] / l_ref[g]
        s_ref[g, :, 0:wlen] = _dot_nt(qss[g], kwin_ref[pl.ds(ws, wlen), gcs[g]])
        for rs in head_rows:
            sc = s_ref[g, rs, 0:wlen] + win_bias
            pe = jnp.exp2((sc - jnp.max(sc, -1, keepdims=True)) * c)
            l_ref[g, rs, :] = jnp.sum(pe, -1, keepdims=True)
            p_ref[g, rs, 0:wlen] = pe.astype(BF16)
        o_win = _dot(p_ref[g, :, 0:wlen], vwin_ref[pl.ds(ws, wlen), gcs[g]]) / l_ref[g]
        for h, rs in enumerate(head_rows):
            hh = g * NSA_HG + h
            c0 = SSM_HEADS + 3 * hh
            o = (gates[:, c0:c0 + 1] * ocmp_ref[g, rs, :] + gates[:, c0 + 1:c0 + 2] * o_sel[rs]
                 + gates[:, c0 + 2:c0 + 3] * o_win[rs])
            o_ref[:, hh * dh:(hh + 1) * dh] = o.astype(o_ref.dtype)


def _nsa_mixer(q, kv, small, kvc):
    bsz, seq, _ = q.shape
    tq = NSA_Q_TILE
    ncp = seq // NSA_CMP_STRIDE
    n_sel = seq // NSA_SEL_BLOCK
    wlen = NSA_WINDOW + tq
    assert seq % NSA_SEL_K_TILE == 0 and seq >= wlen and n_sel <= LANES and tq == LANES
    swidth = max(wlen, NSA_SEL_K_TILE, ncp)
    sj = np.arange(LANES)[:, None]
    ci = np.arange(ncp)[None, :]
    ovt = np.clip(np.minimum(ci * NSA_CMP_STRIDE + NSA_CMP_BLOCK, sj * NSA_SEL_BLOCK + NSA_SEL_BLOCK)
                  - np.maximum(ci * NSA_CMP_STRIDE, sj * NSA_SEL_BLOCK), 0, None) / NSA_CMP_BLOCK
    ovt = np.where((ci < ncp - 1) & (sj < n_sel), ovt, 0.0)
    rows = NSA_HG * tq
    kvblk = lambda idx: pl.BlockSpec((None, seq, NSA_KV), lambda b, i: (b, 0, idx))
    return pl.pallas_call(
        functools.partial(_nsa_kernel, seq=seq),
        grid=(bsz, seq // tq),
        in_specs=[
            pl.BlockSpec((None, tq, NSA_WIDTH), lambda b, i: (b, i, 0)),
            pl.BlockSpec((None, 2 * NSA_GROUPS, ncp, NSA_HEAD_DIM), lambda b, i: (b, 0, 0, 0)),
            kvblk(2), kvblk(3), kvblk(4), kvblk(5),
            pl.BlockSpec((None, tq, LANES), lambda b, i: (b, i, 0)),
            pl.BlockSpec((LANES, ncp), lambda b, i: (0, 0)),
        ],
        out_specs=pl.BlockSpec((None, tq, NSA_WIDTH), lambda b, i: (b, i, 0)),
        out_shape=jax.ShapeDtypeStruct((bsz, seq, NSA_WIDTH), BF16),
        scratch_shapes=[pltpu.VMEM((NSA_GROUPS, rows, swidth), F32), pltpu.VMEM((NSA_GROUPS, rows, swidth), BF16),
                        pltpu.VMEM((NSA_GROUPS, tq, swidth), F32),
                        pltpu.VMEM((NSA_GROUPS, rows, 1), F32), pltpu.VMEM((NSA_GROUPS, rows, 1), F32),
                        pltpu.VMEM((NSA_GROUPS, rows, NSA_HEAD_DIM), F32),
                        pltpu.VMEM((NSA_GROUPS, rows, NSA_HEAD_DIM), F32)],
        compiler_params=_cp("parallel", "arbitrary"),
        name="nsa_attention",
    )(q, kvc, kv, kv, kv, kv, small, jnp.asarray(ovt, BF16))


def _mla_prep_kernel(c_ref, kr_ref, pos_ref, qnw_ref, kvnw_ref, inv_ref, cq_ref, ckv_ref, k1_ref, k2_ref,
                     cos_ref, sin_ref):
    def rms(v, w):
        return v * lax.rsqrt(jnp.mean(v * v, -1, keepdims=True) + RMS_EPS) * w

    cq_ref[...] = rms(c_ref[:, 0:MLA_Q_RANK], qnw_ref[...]).astype(BF16)
    ckv_ref[...] = rms(c_ref[:, MLA_Q_RANK:MLA_Q_RANK + MLA_KV_RANK], kvnw_ref[...]).astype(BF16)
    ang = pos_ref[...].astype(F32) * inv_ref[...]
    cos = jnp.cos(ang)
    sin = jnp.sin(ang)
    half = MLA_ROPE // 2
    t1 = kr_ref[:, 0:half]
    t2 = kr_ref[:, half:MLA_ROPE]
    k1_ref[...] = (t1 * cos - t2 * sin).astype(BF16)
    k2_ref[...] = (t1 * sin + t2 * cos).astype(BF16)
    cos_ref[...] = cos
    sin_ref[...] = sin


def _mla_prep(c, kr, pos, q_norm_w, kv_norm_w, tm=512):
    m = c.shape[0]
    half = MLA_ROPE // 2
    inv = (1.0 / (ROPE_THETA ** (jnp.arange(0, MLA_ROPE, 2, dtype=F32) / MLA_ROPE)))[None, :]
    row = lambda w: pl.BlockSpec((tm, w), lambda i: (i, 0))
    vec = lambda w: pl.BlockSpec((1, w), lambda i: (0, 0))
    return pl.pallas_call(
        _mla_prep_kernel,
        grid=(m // tm,),
        in_specs=[row(MLA_Q_RANK + MLA_KV_RANK), row(LANES), row(1), vec(MLA_Q_RANK), vec(MLA_KV_RANK), vec(half)],
        out_specs=[row(MLA_Q_RANK), row(MLA_KV_RANK), row(half), row(half), row(half), row(half)],
        out_shape=[jax.ShapeDtypeStruct((m, MLA_Q_RANK), BF16), jax.ShapeDtypeStruct((m, MLA_KV_RANK), BF16),
                   jax.ShapeDtypeStruct((m, half), BF16), jax.ShapeDtypeStruct((m, half), BF16),
                   jax.ShapeDtypeStruct((m, half), F32), jax.ShapeDtypeStruct((m, half), F32)],
        compiler_params=_cp("parallel"),
        name="mla_prep",
    )(c, kr, pos, q_norm_w[None, :], kv_norm_w[None, :], inv)


def _q_rope_kernel(t_ref, cos_ref, sin_ref, e_ref, r1_ref, r2_ref):
    hw = MLA_HEADS * MLA_ROPE // 2
    cos = _dot_f32_by_exact(cos_ref[...], e_ref[...])
    sin = _dot_f32_by_exact(sin_ref[...], e_ref[...])
    t1 = t_ref[:, 0:hw]
    t2 = t_ref[:, hw:2 * hw]
    r1_ref[...] = (t1 * cos - t2 * sin).astype(BF16)
    r2_ref[...] = (t1 * sin + t2 * cos).astype(BF16)


def _q_rope(t, cos, sin, tm=512):
    m = t.shape[0]
    half = MLA_ROPE // 2
    hw = MLA_HEADS * half
    e = (np.arange(hw)[None, :] % half == np.arange(half)[:, None])
    row = lambda w: pl.BlockSpec((tm, w), lambda i: (i, 0))
    return pl.pallas_call(
        _q_rope_kernel,
        grid=(m // tm,),
        in_specs=[row(2 * hw), row(half), row(half), pl.BlockSpec((half, hw), lambda i: (0, 0))],
        out_specs=[row(hw), row(hw)],
        out_shape=[jax.ShapeDtypeStruct((m, hw), BF16)] * 2,
        compiler_params=_cp("parallel"),
        name="mla_q_rope",
    )(t, cos, sin, jnp.asarray(e, BF16))


def _mla_attn_kernel(qn_ref, qr_ref, kn_ref, kr_ref, v_ref, o_ref, s_ref):
    t = MLA_TILE
    seq = qn_ref.shape[0]
    c = (MLA_NOPE + MLA_ROPE) ** -0.5 * float(np.log2(np.e))
    causal = _iota((t, t), 1) <= _iota((t, t), 0)
    for qi in range(seq // t):
        rows = slice(qi * t, (qi + 1) * t)
        qn = qn_ref[rows, :]
        qr = qr_ref[rows, :]
        for kt in range(qi + 1):
            keys = slice(kt * t, (kt + 1) * t)
            s = _dot_nt(qn, kn_ref[keys, :]) + _dot_nt(qr, kr_ref[keys, :])
            s_ref[:, keys] = jnp.where(causal, s, NEG_INF) if kt == qi else s
        n = (qi + 1) * t
        s = s_ref[:, 0:n]
        p = jnp.exp2((s - jnp.max(s, -1, keepdims=True)) * c)
        o = _dot(p, v_ref[0:n, :]) / jnp.sum(p, -1, keepdims=True)
        o_ref[rows, :] = o.astype(o_ref.dtype)


def _mla_attention(qn, qr, kv, kr):
    bsz, seq, _ = qn.shape
    assert seq % MLA_TILE == 0
    return pl.pallas_call(
        _mla_attn_kernel,
        grid=(bsz, MLA_HEADS),
        in_specs=[
            pl.BlockSpec((None, seq, MLA_NOPE), lambda b, h: (b, 0, h)),
            pl.BlockSpec((None, None, seq, MLA_ROPE), lambda b, h: (b, h, 0, 0)),
            pl.BlockSpec((None, seq, MLA_NOPE), lambda b, h: (b, 0, 2 * h)),
            pl.BlockSpec((None, seq, MLA_ROPE), lambda b, h: (b, 0, 0)),
            pl.BlockSpec((None, seq, MLA_V), lambda b, h: (b, 0, 2 * h + 1)),
        ],
        out_specs=pl.BlockSpec((None, seq, MLA_V), lambda b, h: (b, 0, h)),
        out_shape=jax.ShapeDtypeStruct((bsz, seq, MLA_HEADS * MLA_V), BF16),
        scratch_shapes=[pltpu.VMEM((MLA_TILE, seq), F32)],
        compiler_params=_cp("parallel", "arbitrary"),
        name="mla_attention",
    )(qn, qr, kv, kr, kv)


def _hgrn_kernel(hq_ref, hf_ref, hi_ref, hg_ref, lb_ref, nw_ref, o_ref, st):
    C = HGRN_CHUNK

    @pl.when(pl.program_id(2) == 0)
    def _():
        st[...] = jnp.zeros_like(st)

    low = _iota((C, C), 1) <= _iota((C, C), 0)
    tri = jnp.where(low, 1.0, 0.0)
    for c in range(HGRN_TILE // C):
        rs = slice(c * C, (c + 1) * C)
        for j in range(HGRN_HEADS_PER_STEP):
            cs = slice(j * HGRN_KEY, (j + 1) * HGRN_KEY)
            lb = lb_ref[:, cs]
            hf = hf_ref[rs, cs]
            v = hi_ref[rs, cs]
            q = _silu(hq_ref[rs, cs])
            f = lb + (1.0 - lb) * jax.nn.sigmoid(hf)
            k = (1.0 - lb) * jax.nn.sigmoid(-hf)
            b = _dot_exact_by_f32(tri, jnp.log(f))
            b_last = b[C - 1:C, :]
            q_t = q * jnp.exp(b)
            k_t = k * jnp.exp(-b)
            k_end = k * jnp.exp(b_last - b)
            att = jnp.where(low, _dot_nt(q_t, k_t), 0.0)
            s_t = st[j]
            o = _dot(att, v) + _dot_nt(q_t, s_t)
            st[j] = s_t * jnp.exp(b_last) + _dot(v.T, k_end)
            o = o * lax.rsqrt(jnp.mean(o * o, -1, keepdims=True) + RMS_EPS) * nw_ref[...]
            o_ref[rs, cs] = (o * _silu(hg_ref[rs, cs])).astype(o_ref.dtype)


def _hgrn_mixer(hp, lb, norm_w):
    bsz, seq, _ = hp.shape
    t = HGRN_TILE
    hps = HGRN_HEADS_PER_STEP
    w = hps * HGRN_KEY
    nh = HGRN_HEADS // hps
    part = lambda p: pl.BlockSpec((None, t, w), lambda b, h, i: (b, i, p * nh + h))
    return pl.pallas_call(
        _hgrn_kernel,
        grid=(bsz, nh, seq // t),
        in_specs=[part(0), part(1), part(2), part(3),
                  pl.BlockSpec((None, 1, w), lambda b, h, i: (h, 0, 0)),
                  pl.BlockSpec((1, HGRN_VAL), lambda b, h, i: (0, 0))],
        out_specs=pl.BlockSpec((None, t, w), lambda b, h, i: (b, i, h)),
        out_shape=jax.ShapeDtypeStruct((bsz, seq, HGRN_WIDTH), BF16),
        scratch_shapes=[pltpu.VMEM((hps, HGRN_VAL, HGRN_KEY), F32)],
        compiler_params=_cp("parallel", "parallel", "arbitrary"),
        name="hgrn2_mixer",
    )(hp, hp, hp, hp, lb.reshape(nh, 1, w), norm_w[None, :])


def _router_kernel(x_ref, wt_ref, bias_ref, esel_ref, rank_ref, wsel_ref, cnt_ref, carry):
    tm = x_ref.shape[0]
    n_grp = N_EXPERT_GROUPS
    gsz = N_EXPERTS // n_grp
    ninf = -jnp.inf

    @pl.when(pl.program_id(0) == 0)
    def _():
        carry[...] = jnp.zeros_like(carry)

    x = x_ref[...]
    w = wt_ref[...]
    xh = x.astype(BF16)
    xl = (x - xh.astype(F32)).astype(BF16)
    wh = w.astype(BF16)
    wl = (w - wh.astype(F32)).astype(BF16)
    dnt = lambda a, b: lax.dot_general(a, b, (((1,), (1,)), ((), ())), preferred_element_type=F32)
    scores = jax.nn.sigmoid(dnt(wh, xh) + dnt(wh, xl) + dnt(wl, xh))
    biased = scores + bias_ref[...]
    sub = _iota((gsz, tm), 0).astype(F32)

    def top1(blk):
        mx = jnp.max(blk, 0, keepdims=True)
        return mx, jnp.min(jnp.where(blk == mx, sub, 1e9), 0, keepdims=True)

    blocks = [biased[g * gsz:(g + 1) * gsz, :] for g in range(n_grp)]
    grp = []
    for blk in blocks:
        m1, i1 = top1(blk)
        m2, _ = top1(jnp.where(sub == i1, ninf, blk))
        grp.append(m1 + m2)
    grp = jnp.concatenate(grp, axis=0)
    gsel = jnp.zeros((n_grp, tm), F32)
    for _ in range(TOP_GROUPS):
        _, first = top1(grp)
        hit = sub == first
        gsel = jnp.where(hit, 1.0, gsel)
        grp = jnp.where(hit, ninf, grp)

    cand = [jnp.where(gsel[g:g + 1, :] > 0.5, blocks[g], ninf) for g in range(n_grp)]
    eid = [sub + float(g * gsz) for g in range(n_grp)]
    sel = [jnp.zeros((gsz, tm), F32) for _ in range(n_grp)]
    firsts = []
    for _ in range(TOP_K):
        mm = cand[0]
        for g in range(1, n_grp):
            mm = jnp.maximum(mm, cand[g])
        mx = jnp.max(mm, 0, keepdims=True)
        fm = jnp.where(cand[0] == mx, eid[0], 1e9)
        for g in range(1, n_grp):
            fm = jnp.minimum(fm, jnp.where(cand[g] == mx, eid[g], 1e9))
        first = jnp.min(fm, 0, keepdims=True)
        firsts.append(first)
        for g in range(n_grp):
            hit = eid[g] == first
            sel[g] = jnp.where(hit, 1.0, sel[g])
            cand[g] = jnp.where(hit, ninf, cand[g])

    selm = jnp.concatenate(sel, axis=0)
    before = jnp.where(_iota((tm, tm), 0) < _iota((tm, tm), 1), 1.0, 0.0).astype(BF16)
    rank = jnp.dot(selm.astype(BF16), before, preferred_element_type=F32) + carry[...]
    carry[...] = carry[...] + jnp.sum(selm, -1, keepdims=True)
    cnt_ref[...] = carry[...]

    ws, rs = [], []
    for first in firsts:
        wacc = jnp.zeros((gsz, tm), F32)
        racc = jnp.zeros((gsz, tm), F32)
        for g in range(n_grp):
            hit = eid[g] == first
            wacc = jnp.where(hit, scores[g * gsz:(g + 1) * gsz, :], wacc)
            racc = jnp.where(hit, rank[g * gsz:(g + 1) * gsz, :], racc)
        ws.append(jnp.sum(wacc, 0, keepdims=True))
        rs.append(jnp.sum(racc, 0, keepdims=True))
    wsum = ws[0]
    for wk in ws[1:]:
        wsum = wsum + wk
    esel_ref[...] = jnp.concatenate(firsts, axis=0).astype(jnp.int32)
    rank_ref[...] = jnp.concatenate(rs, axis=0).astype(jnp.int32)
    wsel_ref[...] = jnp.concatenate(ws, axis=0) / wsum * ROUTED_SCALE


def _route(x, w_router, router_bias, tm=256):
    T, d = x.shape
    E = N_EXPERTS
    assert N_EXPERT_GROUPS == E // N_EXPERT_GROUPS == 8
    kt = pl.BlockSpec((TOP_K, tm), lambda i: (0, i))
    return pl.pallas_call(
        _router_kernel,
        grid=(T // tm,),
        in_specs=[pl.BlockSpec((tm, d), lambda i: (i, 0)), pl.BlockSpec((E, d), lambda i: (0, 0)),
                  pl.BlockSpec((E, 1), lambda i: (0, 0))],
        out_specs=[kt, kt, kt, pl.BlockSpec((E, 1), lambda i: (0, 0))],
        out_shape=[jax.ShapeDtypeStruct((TOP_K, T), jnp.int32), jax.ShapeDtypeStruct((TOP_K, T), jnp.int32),
                   jax.ShapeDtypeStruct((TOP_K, T), F32), jax.ShapeDtypeStruct((E, 1), F32)],
        scratch_shapes=[pltpu.VMEM((E, 1), F32)],
        compiler_params=_cp("arbitrary"),
        name="moe_router",
    )(x, w_router.T, router_bias.astype(F32)[:, None])


def _expert_kernel(be_ref, nv_ref, x_ref, wg_ref, wu_ref, wd_ref, o_ref, wg_b, wu_b, wd_b):
    i = pl.program_id(0)

    @pl.when((i == 0) | (be_ref[i] != be_ref[jnp.maximum(i - 1, 0)]))
    def _():
        wg_b[...] = wg_ref[...].astype(BF16)
        wu_b[...] = wu_ref[...].astype(BF16)
        wd_b[...] = wd_ref[...].astype(BF16)

    @pl.when(i < nv_ref[0])
    def _():
        x = x_ref[...]
        h = _silu(jnp.dot(x, wg_b[...], preferred_element_type=F32)) * jnp.dot(
            x, wu_b[...], preferred_element_type=F32)
        o_ref[...] = _pack_bf16_pairs(jnp.dot(h.astype(BF16), wd_b[...], preferred_element_type=F32))

    @pl.when(i >= nv_ref[0])
    def _():
        o_ref[...] = jnp.zeros_like(o_ref)


def _pack_bf16_pairs(y):
    half = y.shape[1] // 2
    bits = lambda v: lax.bitcast_convert_type(v.astype(BF16).astype(F32), jnp.uint32)
    return bits(y[:, :half]) | (bits(y[:, half:]) >> 16)


def _unpack_bf16_pairs(u):
    hi = lax.bitcast_convert_type(u & jnp.uint32(0xFFFF0000), F32)
    lo = lax.bitcast_convert_type(u << 16, F32)
    return hi, lo


def _expert_ffn(x_rows, blk_e, n_valid, w_gate, w_up, w_down, layer, mb):
    p, d = x_rows.shape
    ff = w_gate.shape[3]
    xmap = lambda i, be, nv: (jnp.minimum(i, nv[0] - 1), 0)
    wmap = lambda i, be, nv: (layer, be[i], 0, 0)
    grid_spec = pltpu.PrefetchScalarGridSpec(
        num_scalar_prefetch=2,
        grid=(p // mb,),
        in_specs=[
            pl.BlockSpec((mb, d), xmap),
            pl.BlockSpec((None, None, d, ff), wmap), pl.BlockSpec((None, None, d, ff), wmap),
            pl.BlockSpec((None, None, ff, d), wmap),
        ],
        out_specs=pl.BlockSpec((mb, d // 2), lambda i, be, nv: (i, 0)),
        scratch_shapes=[pltpu.VMEM((d, ff), BF16), pltpu.VMEM((d, ff), BF16), pltpu.VMEM((ff, d), BF16)],
    )
    return pl.pallas_call(
        _expert_kernel,
        grid_spec=grid_spec,
        out_shape=jax.ShapeDtypeStruct((p, d // 2), jnp.uint32),
        compiler_params=pltpu.CompilerParams(dimension_semantics=("arbitrary",), vmem_limit_bytes=EXPERT_VMEM_LIMIT),
        name="moe_expert_ffn",
    )(blk_e, n_valid, x_rows, w_gate, w_up, w_down)


def _moe_ffn(x, x_bf, layer, w_router, router_bias, w_gate, w_up, w_down, ws_gate, ws_up, ws_down):
    T, d = x.shape
    E = N_EXPERTS
    M = MOE_ROW_BLOCK
    esel, rank, wsel, cnt = _route(x, w_router, router_bias)
    counts = cnt[:, 0].astype(jnp.int32)
    pcounts = (counts + M - 1) // M * M
    pends = jnp.cumsum(pcounts)
    pstarts = pends - pcounts
    NB = -(-(T * TOP_K) // M) + E
    blk_e = jnp.minimum(jnp.sum(pends[None, :] <= (jnp.arange(NB, dtype=jnp.int32) * M)[:, None], axis=1),
                        E - 1).astype(jnp.int32)
    n_valid = (pends[-1] // M).astype(jnp.int32).reshape(1)
    slot = rank + jnp.sum(jnp.where(esel[:, :, None] == jnp.arange(E, dtype=jnp.int32), pstarts, 0), axis=-1)
    row_tok = jnp.zeros((NB * M,), jnp.int32).at[slot.reshape(TOP_K * T)].set(
        jnp.tile(jnp.arange(T, dtype=jnp.int32), TOP_K), unique_indices=True)
    y_rows = _expert_ffn(x_bf[row_tok], blk_e, n_valid, w_gate, w_up, w_down, layer, M)

    depth = ws_gate.shape[0]
    ms = SHARED_ROW_BLOCK
    shared = _expert_ffn(x_bf, jnp.zeros((T // ms,), jnp.int32), jnp.full((1,), T // ms, jnp.int32),
                         ws_gate.reshape(depth, 1, d, -1), ws_up.reshape(depth, 1, d, -1),
                         ws_down.reshape(depth, 1, -1, d), layer, ms)
    return shared, y_rows, slot, wsel.T


def _mixer_ab(x_bf, bsz, seq, w_in, conv_w, conv_b, dt_bias, a_log, d_skip, norm_w, cmp_pos, cmp_w, w_out):
    T = bsz * seq
    o_zx = SSM_INNER + SSM_CONV_DIM
    o_dt = o_zx + SSM_HEADS
    o_q = o_dt + NSA_WIDTH
    o_kv = o_q + 6 * NSA_KV
    w_bf = w_in.astype(BF16)
    zx = _matmul(x_bf, w_bf[:, :o_zx], F32)
    w_small = _pad_cols(jnp.concatenate([w_bf[:, o_zx:o_dt], w_bf[:, o_kv:]], axis=1), LANES)
    small = _matmul(x_bf, w_small, F32)
    q = _matmul(x_bf, w_bf[:, o_dt:o_q], BF16)
    kv = _matmul(x_bf, w_bf[:, o_q:o_kv], BF16)

    zx3 = zx.reshape(bsz, seq, o_zx)
    small3 = small.reshape(bsz, seq, LANES)
    dt_t = jnp.swapaxes(small3[:, :, :SSM_HEADS], 1, 2)
    y_a = _ssd_mixer(zx3, small3, dt_t, conv_w, conv_b, dt_bias, a_log, d_skip, norm_w)

    kv3 = kv.reshape(bsz, seq, 6 * NSA_KV)
    ncp = seq // NSA_CMP_STRIDE
    strips = kv3[:, :, :2 * NSA_KV].reshape(bsz, ncp, NSA_CMP_STRIDE, 2 * NSA_GROUPS, NSA_HEAD_DIM)
    strips = strips.transpose(0, 3, 1, 2, 4).reshape(bsz, 2 * NSA_GROUPS, ncp, NSA_CMP_STRIDE * NSA_HEAD_DIM)
    kvc = _nsa_compress(strips, cmp_w, cmp_pos)
    y_b = _nsa_mixer(q.reshape(bsz, seq, NSA_WIDTH), kv3, small3, kvc)

    return _matmul_cat(y_a.reshape(T, SSM_INNER), y_b.reshape(T, NSA_WIDTH), w_out.astype(BF16), F32)


def _mixer_cd(x_bf, bsz, seq, positions, w_in, q_norm_w, w_uq, kv_norm_w, w_ukv, lb, hgrn_norm_w, w_out):
    T = bsz * seq
    o_c = MLA_Q_RANK + MLA_KV_RANK
    o_kr = o_c + MLA_ROPE
    half = MLA_ROPE // 2
    w_bf = w_in.astype(BF16)
    c = _matmul(x_bf, w_bf[:, :o_c], F32)
    kr = _matmul(x_bf, _pad_cols(w_bf[:, o_c:o_kr], LANES), F32)
    hp = _matmul(x_bf, w_bf[:, o_kr:], F32)

    cqn, ckvn, k1, k2, cos, sin = _mla_prep(c, kr, positions.reshape(T, 1), q_norm_w, kv_norm_w)
    wq = w_uq.astype(BF16).reshape(MLA_Q_RANK, MLA_HEADS, MLA_NOPE + MLA_ROPE)
    w_nope = wq[:, :, :MLA_NOPE].reshape(MLA_Q_RANK, MLA_HEADS * MLA_NOPE)
    w_rope = jnp.concatenate([wq[:, :, MLA_NOPE:MLA_NOPE + half].reshape(MLA_Q_RANK, MLA_HEADS * half),
                              wq[:, :, MLA_NOPE + half:].reshape(MLA_Q_RANK, MLA_HEADS * half)], axis=1)
    qn = _matmul(cqn, w_nope, BF16)
    r1, r2 = _q_rope(_matmul(cqn, w_rope, F32), cos, sin)
    qr = jnp.concatenate([r1.reshape(bsz, seq, MLA_HEADS, half), r2.reshape(bsz, seq, MLA_HEADS, half)], axis=-1)
    qr = qr.transpose(0, 2, 1, 3)
    kvu = _matmul(ckvn, w_ukv.astype(BF16), BF16)
    y_c = _mla_attention(qn.reshape(bsz, seq, -1), qr, kvu.reshape(bsz, seq, -1),
                         jnp.concatenate([k1, k2], axis=1).reshape(bsz, seq, MLA_ROPE))
    y_d = _hgrn_mixer(hp.reshape(bsz, seq, -1), lb, hgrn_norm_w)
    return _matmul_cat(y_c.reshape(T, -1), y_d.reshape(T, -1), w_out.astype(BF16), F32)


def kernel(x, positions, ab_w_in, ssm_conv_w, ssm_conv_b, ssm_dt_bias, ssm_a_log, ssm_d, ssm_norm_w, nsa_cmp_pos, nsa_cmp_w, ab_w_out, cd_w_in, mla_q_norm_w, mla_w_uq, mla_kv_norm_w, mla_w_ukv, hgrn_lb_logits, hgrn_norm_w, cd_w_out, ln_g, ln_b, moe_w_router, moe_router_bias, moe_w_gate, moe_w_up, moe_w_down, moe_shared_w_gate, moe_shared_w_up, moe_shared_w_down):
    bsz, seq, d = x.shape
    T = bsz * seq
    lb_all = jnp.cumsum(jax.nn.softmax(hgrn_lb_logits.astype(F32), axis=0), axis=0)
    lb_all = lb_all - lb_all[0]
    xf = x.reshape(T, d)
    x_bf = xf.astype(BF16)
    for l in range(DEPTH):
        i = l // 2
        if l % 2 == 0:
            mix = _mixer_ab(x_bf, bsz, seq, ab_w_in[i], ssm_conv_w[i], ssm_conv_b[i], ssm_dt_bias[i], ssm_a_log[i],
                            ssm_d[i], ssm_norm_w[i], nsa_cmp_pos[i], nsa_cmp_w[i], ab_w_out[i])
        else:
            mix = _mixer_cd(x_bf, bsz, seq, positions, cd_w_in[i], mla_q_norm_w[i], mla_w_uq[i], mla_kv_norm_w[i],
                            mla_w_ukv[i], lb_all[l], hgrn_norm_w[i], cd_w_out[i])
        xf, x_bf = _deepnorm_ln(xf, [mix], ln_g[l, 0], ln_b[l, 0])
        shared, y_rows, slot, w_k = _moe_ffn(xf, x_bf, l, moe_w_router[l], moe_router_bias[l], moe_w_gate,
                                             moe_w_up, moe_w_down, moe_shared_w_gate, moe_shared_w_up,
                                             moe_shared_w_down)
        xf, x_bf = _deepnorm_ln_combine(xf, shared, y_rows, slot, w_k, ln_g[l, 1], ln_b[l, 1])
    return xf.reshape(bsz, seq, d)
```

```python
import functools

import numpy as np
import jax
import jax.numpy as jnp
from jax import lax
from jax.experimental import pallas as pl
from jax.experimental.pallas import tpu as pltpu

F32 = jnp.float32
BF16 = jnp.bfloat16

D_MODEL = 2048
DEPTH = 2
DN_ALPHA = float((2 * DEPTH) ** 0.25)
LN_EPS = 1e-5
RMS_EPS = 1e-6
NEG_INF = -1e30

SSM_HEAD_DIM = 64
SSM_INNER = D_MODEL
SSM_HEADS = SSM_INNER // SSM_HEAD_DIM
SSM_STATE = 128
SSM_GROUPS = 4
SSM_CONV = 4
SSM_CHUNK = 128
SSM_BC = SSM_GROUPS * SSM_STATE
SSM_CONV_DIM = SSM_INNER + 2 * SSM_BC
SSM_GROUP_W = SSM_INNER // SSM_GROUPS

NSA_HEADS = 16
NSA_GROUPS = 2
NSA_HG = NSA_HEADS // NSA_GROUPS
NSA_HEAD_DIM = D_MODEL // NSA_HEADS
NSA_WIDTH = NSA_HEADS * NSA_HEAD_DIM
NSA_KV = NSA_GROUPS * NSA_HEAD_DIM
NSA_CMP_BLOCK = 32
NSA_CMP_STRIDE = 16
NSA_SEL_BLOCK = 64
NSA_TOP_N = 8
NSA_WINDOW = 512
NSA_FORCED = 1e6
NSA_Q_TILE = 128
NSA_SEL_K_TILE = 512

MLA_HEADS = 16
MLA_Q_RANK = 512
MLA_KV_RANK = 512
MLA_NOPE = 128
MLA_ROPE = 64
MLA_V = 128
ROPE_THETA = 10000.0
MLA_TILE = 512

HGRN_HEADS = 16
HGRN_KEY = 128
HGRN_VAL = 128
HGRN_CHUNK = 64
HGRN_FDIM = HGRN_HEADS * HGRN_KEY
HGRN_WIDTH = HGRN_HEADS * HGRN_VAL
HGRN_TILE = 512
HGRN_HEADS_PER_STEP = 2

N_EXPERTS = 64
TOP_K = 8
N_EXPERT_GROUPS = 8
TOP_GROUPS = 4
EXPERT_FF = 512
ROUTED_SCALE = 2.5
MOE_ROW_BLOCK = 512
SHARED_ROW_BLOCK = 512

LANES = 128
VMEM_LIMIT = 48 * 1024 * 1024
EXPERT_VMEM_LIMIT = 56 * 1024 * 1024


def _cp(*sem):
    return pltpu.CompilerParams(dimension_semantics=sem, vmem_limit_bytes=VMEM_LIMIT)


def _dot(a, b):
    return jnp.dot(a.astype(BF16), b.astype(BF16), preferred_element_type=F32)


def _dot_nt(a, b):
    return lax.dot_general(a.astype(BF16), b.astype(BF16), (((1,), (1,)), ((), ())),
                           preferred_element_type=F32)


def _split3(a):
    a1 = a.astype(BF16)
    r1 = a - a1.astype(F32)
    a2 = r1.astype(BF16)
    r2 = r1 - a2.astype(F32)
    return a1, a2, r2.astype(BF16)


def _dot_f32_by_exact(a, t):
    t = t.astype(BF16)
    return sum(jnp.dot(p, t, preferred_element_type=F32) for p in _split3(a))


def _dot_exact_by_f32(t, a):
    t = t.astype(BF16)
    return sum(jnp.dot(t, p, preferred_element_type=F32) for p in _split3(a))


def _silu(x):
    return x * jax.nn.sigmoid(x)


def _softplus(x):
    return jnp.maximum(x, 0.0) + jnp.log(1.0 + jnp.exp(-jnp.abs(x)))


def _iota(shape, dim):
    return lax.broadcasted_iota(jnp.int32, shape, dim)


def _mm_kernel(x_ref, w_ref, o_ref):
    o_ref[...] = jnp.dot(x_ref[...], w_ref[...], preferred_element_type=F32).astype(o_ref.dtype)


def _mm2_kernel(xa_ref, xb_ref, w_ref, o_ref):
    ka = xa_ref.shape[1]
    acc = jnp.dot(xa_ref[...], w_ref[0:ka, :], preferred_element_type=F32)
    acc = acc + jnp.dot(xb_ref[...], w_ref[ka:, :], preferred_element_type=F32)
    o_ref[...] = acc.astype(o_ref.dtype)


def _pick_tile(n, pref):
    if n <= pref:
        return n
    t = pref - pref % LANES
    while n % t:
        t -= LANES
    return t


def _matmul(x, w, out_dtype, tm=2048, tn=512):
    m, k = x.shape
    n = w.shape[1]
    tm = _pick_tile(m, tm)
    tn = _pick_tile(n, tn)
    return pl.pallas_call(
        _mm_kernel,
        grid=(m // tm, n // tn),
        in_specs=[pl.BlockSpec((tm, k), lambda i, j: (i, 0)), pl.BlockSpec((k, tn), lambda i, j: (0, j))],
        out_specs=pl.BlockSpec((tm, tn), lambda i, j: (i, j)),
        out_shape=jax.ShapeDtypeStruct((m, n), out_dtype),
        compiler_params=_cp("parallel", "arbitrary"),
        name="matmul",
    )(x, w)


def _matmul_cat(xa, xb, w, out_dtype, tm=1024, tn=1024):
    m, ka = xa.shape
    kb = xb.shape[1]
    n = w.shape[1]
    tm = _pick_tile(m, tm)
    tn = _pick_tile(n, tn)
    return pl.pallas_call(
        _mm2_kernel,
        grid=(m // tm, n // tn),
        in_specs=[pl.BlockSpec((tm, ka), lambda i, j: (i, 0)), pl.BlockSpec((tm, kb), lambda i, j: (i, 0)),
                  pl.BlockSpec((ka + kb, tn), lambda i, j: (0, j))],
        out_specs=pl.BlockSpec((tm, tn), lambda i, j: (i, j)),
        out_shape=jax.ShapeDtypeStruct((m, n), out_dtype),
        compiler_params=_cp("parallel", "arbitrary"),
        name="matmul_cat",
    )(xa, xb, w)


def _pad_cols(w, mult):
    n = w.shape[1]
    pad = (-n) % mult
    return jnp.pad(w, ((0, 0), (0, pad))) if pad else w


def _ln_combine_kernel(x_ref, sh_ref, y_ref, w_ref, g_ref, b_ref, o_ref, obf_ref):
    hi, lo = _unpack_bf16_pairs(sh_ref[...])
    w = w_ref[...]
    for k in range(TOP_K):
        yh, yl = _unpack_bf16_pairs(y_ref[k])
        hi = hi + w[:, k:k + 1] * yh
        lo = lo + w[:, k:k + 1] * yl
    v = DN_ALPHA * x_ref[...] + jnp.concatenate([hi, lo], axis=1)
    mu = jnp.mean(v, -1, keepdims=True)
    d = v - mu
    var = jnp.mean(d * d, -1, keepdims=True)
    y = d * lax.rsqrt(var + LN_EPS) * g_ref[...] + b_ref[...]
    o_ref[...] = y
    obf_ref[...] = y.astype(BF16)


def _deepnorm_ln_combine(x, shared, y_k, w_k, g, b, tm=128):
    m, d = x.shape
    row = pl.BlockSpec((tm, d), lambda i: (i, 0))
    vec = pl.BlockSpec((1, d), lambda i: (0, 0))
    return pl.pallas_call(
        _ln_combine_kernel,
        grid=(m // tm,),
        in_specs=[row, pl.BlockSpec((tm, d // 2), lambda i: (i, 0)),
                  pl.BlockSpec((TOP_K, tm, d // 2), lambda i: (0, i, 0)),
                  pl.BlockSpec((tm, TOP_K), lambda i: (i, 0)), vec, vec],
        out_specs=[row, row],
        out_shape=[jax.ShapeDtypeStruct((m, d), F32), jax.ShapeDtypeStruct((m, d), BF16)],
        compiler_params=_cp("parallel"),
        name="deepnorm_ln_combine",
    )(x, shared, y_k, w_k, g.reshape(1, d), b.reshape(1, d))


def _ln_kernel(x_ref, mix_ref, g_ref, b_ref, o_ref, opk_ref):
    v = DN_ALPHA * x_ref[...] + mix_ref[...]
    mu = jnp.mean(v, -1, keepdims=True)
    d = v - mu
    var = jnp.mean(d * d, -1, keepdims=True)
    y = d * lax.rsqrt(var + LN_EPS) * g_ref[...] + b_ref[...]
    o_ref[...] = y
    opk_ref[...] = _pack_bf16_pairs(y)


def _deepnorm_ln(x, mix, g, b, tm=256):
    m, d = x.shape
    row = pl.BlockSpec((tm, d), lambda i: (i, 0))
    vec = pl.BlockSpec((1, d), lambda i: (0, 0))
    return pl.pallas_call(
        _ln_kernel,
        grid=(m // tm,),
        in_specs=[row, row, vec, vec],
        out_specs=[row, pl.BlockSpec((tm, d // 2), lambda i: (i, 0))],
        out_shape=[jax.ShapeDtypeStruct((m, d), F32), jax.ShapeDtypeStruct((m, d // 2), jnp.uint32)],
        compiler_params=_cp("parallel"),
        name="deepnorm_ln",
    )(x, mix, g.reshape(1, d), b.reshape(1, d))


def _ssd_kernel(z_ref, xs_ref, bc_ref, dt_ref, dtt_ref, cwx_ref, cbx_ref, cwb_ref, cbb_ref, dtb_ref, dtbt_ref,
                alog_ref, alogt_ref, dexp_ref, nw_ref, e_ref, o_ref, extx, extb, st, y_acc):
    L = SSM_CHUNK
    halo = 8

    @pl.when(pl.program_id(1) == 0)
    def _():
        extx[0:halo, :] = jnp.zeros((halo, SSM_INNER), F32)
        extb[0:halo, :] = jnp.zeros((halo, 2 * SSM_BC), F32)
        st[...] = jnp.zeros_like(st)

    extx[halo:halo + L, :] = xs_ref[...]
    extb[halo:halo + L, :] = bc_ref[...]

    def conv(ext, w_ref, b_ref):
        acc = b_ref[...]
        for k in range(SSM_CONV):
            acc = acc + ext[pl.ds(halo - (SSM_CONV - 1) + k, L), :] * w_ref[k:k + 1, :]
        return acc

    xs = _silu(conv(extx, cwx_ref, cbx_ref))
    bc = _silu(conv(extb, cwb_ref, cbb_ref))
    extx[0:halo, :] = xs_ref[L - halo:L, :]
    extb[0:halo, :] = bc_ref[L - halo:L, :]

    li = _iota((L, L), 0)
    si = _iota((L, L), 1)
    low = si <= li
    dt = _softplus(dt_ref[:, 0:SSM_HEADS] + dtb_ref[...])
    adt = dt * (-jnp.exp(alog_ref[...]))
    a_cs = _dot_exact_by_f32(jnp.where(low, 1.0, 0.0), adt)
    dtt = _softplus(dtt_ref[...] + dtbt_ref[...])
    adtt = dtt * (-jnp.exp(alogt_ref[...]))
    a_cst = _dot_f32_by_exact(adtt, jnp.where(li <= si, 1.0, 0.0))

    e = e_ref[...]
    dt_e = _dot_f32_by_exact(dt, e)
    acs_e = _dot_f32_by_exact(a_cs, e)
    a_last = acs_e[L - 1:L, :]
    x_dt = xs * dt_e
    xw = (x_dt * jnp.exp(a_last - acs_e)).astype(BF16)
    ea = jnp.exp(acs_e)
    chunk_decay = jnp.exp(a_last)
    x_dt_b = x_dt.astype(BF16)
    lane = _iota((L, 2 * SSM_HEAD_DIM), 1)

    for g in range(SSM_GROUPS):
        gc = slice(g * SSM_GROUP_W, (g + 1) * SSM_GROUP_W)
        bm = bc[:, g * SSM_STATE:(g + 1) * SSM_STATE]
        cm = bc[:, SSM_BC + g * SSM_STATE:SSM_BC + (g + 1) * SSM_STATE]
        cb = _dot_nt(cm, bm)
        st_g = st[:, gc]
        y_acc[:, gc] = _dot(cm, st_g) * ea[:, gc]
        for pr in range(SSM_GROUP_W // (2 * SSM_HEAD_DIM)):
            h0 = (g * SSM_GROUP_W) // SSM_HEAD_DIM + 2 * pr
            cols = slice(h0 * SSM_HEAD_DIM, (h0 + 2) * SSM_HEAD_DIM)
            xr = x_dt_b[:, cols]
            res = []
            for h in (h0, h0 + 1):
                seg = a_cs[:, h:h + 1] - a_cst[h:h + 1, :]
                res.append(_dot(cb * jnp.where(low, jnp.exp(seg), 0.0), xr))
            y_acc[:, cols] += jnp.where(lane < SSM_HEAD_DIM, res[0], res[1])
        st[:, gc] = st_g * chunk_decay[:, gc] + _dot(bm.T, xw[:, gc])

    y = y_acc[...] + xs * dexp_ref[...]
    y = y * _silu(z_ref[...])
    var = jnp.mean(y * y, -1, keepdims=True)
    o_ref[...] = (y * lax.rsqrt(var + RMS_EPS) * nw_ref[...]).astype(o_ref.dtype)


def _ssd_mixer(zx, small, dt_t, conv_w, conv_b, dt_bias, a_log, d_skip, norm_w):
    bsz, seq, _ = zx.shape
    L = SSM_CHUNK
    nblk = SSM_INNER // (2 * SSM_BC)
    e = (np.arange(SSM_INNER)[None, :] // SSM_HEAD_DIM == np.arange(SSM_HEADS)[:, None])
    full = lambda shape: pl.BlockSpec(shape, lambda b, c: (0,) * len(shape))
    return pl.pallas_call(
        _ssd_kernel,
        grid=(bsz, seq // L),
        in_specs=[
            pl.BlockSpec((None, L, SSM_INNER), lambda b, c: (b, c, 0)),
            pl.BlockSpec((None, L, SSM_INNER), lambda b, c: (b, c, 1)),
            pl.BlockSpec((None, L, 2 * SSM_BC), lambda b, c: (b, c, 2 * nblk)),
            pl.BlockSpec((None, L, LANES), lambda b, c: (b, c, 0)),
            pl.BlockSpec((None, SSM_HEADS, L), lambda b, c: (b, 0, c)),
            full((SSM_CONV, SSM_INNER)), full((1, SSM_INNER)),
            full((SSM_CONV, 2 * SSM_BC)), full((1, 2 * SSM_BC)),
            full((1, SSM_HEADS)), full((SSM_HEADS, 1)), full((1, SSM_HEADS)), full((SSM_HEADS, 1)),
            full((1, SSM_INNER)), full((1, SSM_INNER)), full((SSM_HEADS, SSM_INNER)),
        ],
        out_specs=pl.BlockSpec((None, L, SSM_INNER), lambda b, c: (b, c, 0)),
        out_shape=jax.ShapeDtypeStruct((bsz, seq, SSM_INNER), BF16),
        scratch_shapes=[
            pltpu.VMEM((L + 8, SSM_INNER), F32), pltpu.VMEM((L + 8, 2 * SSM_BC), F32),
            pltpu.VMEM((SSM_STATE, SSM_INNER), F32), pltpu.VMEM((L, SSM_INNER), F32),
        ],
        compiler_params=_cp("arbitrary", "arbitrary"),
        name="ssd_mixer",
    )(zx, zx, zx, small, dt_t,
      conv_w[:, :SSM_INNER], conv_b[None, :SSM_INNER], conv_w[:, SSM_INNER:], conv_b[None, SSM_INNER:],
      dt_bias[None, :], dt_bias[:, None], a_log[None, :], a_log[:, None],
      jnp.repeat(d_skip, SSM_HEAD_DIM)[None, :], norm_w[None, :], jnp.asarray(e, BF16))


def _nsa_compress_kernel(r_ref, w_ref, pos_ref, o_ref):
    ncp = r_ref.shape[0]
    half = NSA_CMP_STRIDE * NSA_HEAD_DIM
    r = r_ref[...].astype(F32)
    top = _dot(r + pos_ref[0:1, :], w_ref[0:half, :])
    bot = _dot(r + pos_ref[1:2, :], w_ref[half:2 * half, :])
    kc = top + pltpu.roll(bot, ncp - 1, axis=0)
    o_ref[...] = jnp.where(_iota(kc.shape, 0) < ncp - 1, kc, 0.0).astype(o_ref.dtype)


def _nsa_compress(strips, cmp_w, cmp_pos):
    bsz, _, ncp, width = strips.shape
    return pl.pallas_call(
        _nsa_compress_kernel,
        grid=(bsz, 2 * NSA_GROUPS),
        in_specs=[
            pl.BlockSpec((None, None, ncp, width), lambda b, j: (b, j, 0, 0)),
            pl.BlockSpec((None, 2 * width, NSA_HEAD_DIM), lambda b, j: (j // NSA_GROUPS, 0, 0)),
            pl.BlockSpec((None, 2, width), lambda b, j: (j // NSA_GROUPS, 0, 0)),
        ],
        out_specs=pl.BlockSpec((None, None, ncp, NSA_HEAD_DIM), lambda b, j: (b, j, 0, 0)),
        out_shape=jax.ShapeDtypeStruct((bsz, 2 * NSA_GROUPS, ncp, NSA_HEAD_DIM), BF16),
        compiler_params=_cp("parallel", "arbitrary"),
        name="nsa_compress",
    )(strips, cmp_w.astype(BF16), cmp_pos.reshape(2, 2, width))


def _nsa_kernel(q_ref, kvc_ref, ksel_ref, vsel_ref, kwin_ref, vwin_ref, gate_ref, ovt_ref, o_ref,
                s_ref, p_ref, bias_ref, m_ref, l_ref, acc_ref, ocmp_ref, *, seq):
    tq = NSA_Q_TILE
    tk = NSA_SEL_K_TILE
    dh = NSA_HEAD_DIM
    ncp = seq // NSA_CMP_STRIDE
    n_cmp = ncp - 1
    n_sel = seq // NSA_SEL_BLOCK
    top_n = min(NSA_TOP_N, n_sel)
    wlen = NSA_WINDOW + tq
    c = dh ** -0.5 * float(np.log2(np.e))
    q0 = pl.program_id(1) * tq
    head_rows = [slice(h * tq, (h + 1) * tq) for h in range(NSA_HG)]

    gates = jax.nn.sigmoid(gate_ref[...])
    has_cmp = q0 + _iota((tq, 1), 0) >= NSA_CMP_BLOCK - 1
    ci = _iota((tq, ncp), 1)
    cmp_ok = (ci * NSA_CMP_STRIDE + (NSA_CMP_BLOCK - 1) <= q0 + _iota((tq, ncp), 0)) & (ci < n_cmp)
    cmp_bias = jnp.where(cmp_ok, 0.0, NEG_INF)

    groups = range(NSA_GROUPS)
    gcs = [slice(g * dh, (g + 1) * dh) for g in groups]
    qss = [jnp.concatenate([q_ref[:, (g * NSA_HG + h) * dh:(g * NSA_HG + h + 1) * dh] for h in range(NSA_HG)],
                           axis=0) for g in groups]
    selbs = []
    for g in groups:
        s_ref[g, :, 0:ncp] = _dot_nt(qss[g], kvc_ref[g])
        psum = jnp.zeros((tq, ncp), F32)
        for rs in head_rows:
            sc = s_ref[g, rs, 0:ncp] + cmp_bias
            pe = jnp.exp2((sc - jnp.max(sc, -1, keepdims=True)) * c)
            p = jnp.where(has_cmp, pe / jnp.sum(pe, -1, keepdims=True), 0.0)
            psum = psum + p
            p_ref[g, rs, 0:ncp] = p.astype(BF16)
        ocmp_ref[g] = _dot(p_ref[g, :, 0:ncp], kvc_ref[NSA_GROUPS + g])

        ovt = ovt_ref[...]
        imp = sum(lax.dot_general(ovt, part, (((1,), (1,)), ((), ())), preferred_element_type=F32)
                  for part in _split3(psum))
        j = _iota((LANES, tq), 0)
        tok = q0 + _iota((LANES, tq), 1)
        cur = tok >> 6
        forced = (j == 0) | (j == cur) | (j == cur - 1)
        imp = jnp.where(forced, NSA_FORCED, jnp.where(j * NSA_SEL_BLOCK <= tok, imp, -1.0))
        imp = jnp.where(j < n_sel, imp, -2.0)
        jf = j.astype(F32)
        sel = jnp.zeros((LANES, tq), F32)
        for _ in range(top_n):
            mx = jnp.max(imp, 0, keepdims=True)
            first = jnp.min(jnp.where(imp == mx, jf, 1e9), 0, keepdims=True)
            hit = jf == first
            sel = jnp.where(hit, 1.0, sel)
            imp = jnp.where(hit, -3.0, imp)
        selbs.append(sel.T.astype(BF16))

    m_ref[...] = jnp.full(m_ref.shape, NEG_INF, F32)
    l_ref[...] = jnp.zeros(l_ref.shape, F32)
    acc_ref[...] = jnp.zeros(acc_ref.shape, F32)

    def sel_step(kt, carry):
        k0 = pl.multiple_of(kt * tk, tk)
        blk_of_key = (k0 + _iota((LANES, tk), 1)) >> 6
        expand = jnp.where(blk_of_key == _iota((LANES, tk), 0), 1.0, 0.0).astype(BF16)
        causal = k0 + _iota((tq, tk), 1) <= q0 + _iota((tq, tk), 0)
        for g in groups:
            s_ref[g, :, 0:tk] = _dot_nt(qss[g], ksel_ref[pl.ds(k0, tk), gcs[g]])
            picked = jnp.dot(selbs[g], expand, preferred_element_type=F32)
            bias_ref[g, :, 0:tk] = jnp.where((picked > 0.5) & causal, 0.0, NEG_INF)
            for rs in head_rows:
                sc = s_ref[g, rs, 0:tk] + bias_ref[g, :, 0:tk]
                m_old = m_ref[g, rs, :]
                m_new = jnp.maximum(m_old, jnp.max(sc, -1, keepdims=True))
                alpha = jnp.exp2((m_old - m_new) * c)
                pe = jnp.exp2((sc - m_new) * c)
                l_ref[g, rs, :] = alpha * l_ref[g, rs, :] + jnp.sum(pe, -1, keepdims=True)
                acc_ref[g, rs, :] = alpha * acc_ref[g, rs, :]
                m_ref[g, rs, :] = m_new
                p_ref[g, rs, 0:tk] = pe.astype(BF16)
            acc_ref[g] += _dot(p_ref[g, :, 0:tk], vsel_ref[pl.ds(k0, tk), gcs[g]])
        return carry

    lax.fori_loop(0, (q0 + tq + tk - 1) // tk, sel_step, 0)

    ws = pl.multiple_of(jnp.maximum(q0 - NSA_WINDOW, 0), tq)
    kpos = ws + _iota((tq, wlen), 1)
    tpos = q0 + _iota((tq, wlen), 0)
    win_bias = jnp.where((kpos <= tpos) & (kpos > tpos - NSA_WINDOW), 0.0, NEG_INF)
    for g in groups:
        o_sel = acc_ref[g] / l_ref[g]
        s_ref[g, :, 0:wlen] = _dot_nt(qss[g], kwin_ref[pl.ds(ws, wlen), gcs[g]])
        for rs in head_rows:
            sc = s_ref[g, rs, 0:wlen] + win_bias
            pe = jnp.exp2((sc - jnp.max(sc, -1, keepdims=True)) * c)
            l_ref[g, rs, :] = jnp.sum(pe, -1, keepdims=True)
            p_ref[g, rs, 0:wlen] = pe.astype(BF16)
        o_win = _dot(p_ref[g, :, 0:wlen], vwin_ref[pl.ds(ws, wlen), gcs[g]]) / l_ref[g]
        for h, rs in enumerate(head_rows):
            hh = g * NSA_HG + h
            c0 = SSM_HEADS + 3 * hh
            o = (gates[:, c0:c0 + 1] * ocmp_ref[g, rs, :] + gates[:, c0 + 1:c0 + 2] * o_sel[rs]
                 + gates[:, c0 + 2:c0 + 3] * o_win[rs])
            o_ref[:, hh * dh:(hh + 1) * dh] = o.astype(o_ref.dtype)


def _nsa_mixer(q, kv, small, kvc):
    bsz, seq, _ = q.shape
    tq = NSA_Q_TILE
    ncp = seq // NSA_CMP_STRIDE
    n_sel = seq // NSA_SEL_BLOCK
    wlen = NSA_WINDOW + tq
    assert seq % NSA_SEL_K_TILE == 0 and seq >= wlen and n_sel <= LANES and tq == LANES
    swidth = max(wlen, NSA_SEL_K_TILE, ncp)
    sj = np.arange(LANES)[:, None]
    ci = np.arange(ncp)[None, :]
    ovt = np.clip(np.minimum(ci * NSA_CMP_STRIDE + NSA_CMP_BLOCK, sj * NSA_SEL_BLOCK + NSA_SEL_BLOCK)
                  - np.maximum(ci * NSA_CMP_STRIDE, sj * NSA_SEL_BLOCK), 0, None) / NSA_CMP_BLOCK
    ovt = np.where((ci < ncp - 1) & (sj < n_sel), ovt, 0.0)
    rows = NSA_HG * tq
    kvblk = lambda idx: pl.BlockSpec((None, seq, NSA_KV), lambda b, i: (b, 0, idx))
    return pl.pallas_call(
        functools.partial(_nsa_kernel, seq=seq),
        grid=(bsz, seq // tq),
        in_specs=[
            pl.BlockSpec((None, tq, NSA_WIDTH), lambda b, i: (b, i, 0)),
            pl.BlockSpec((None, 2 * NSA_GROUPS, ncp, NSA_HEAD_DIM), lambda b, i: (b, 0, 0, 0)),
            kvblk(2), kvblk(3), kvblk(4), kvblk(5),
            pl.BlockSpec((None, tq, LANES), lambda b, i: (b, i, 0)),
            pl.BlockSpec((LANES, ncp), lambda b, i: (0, 0)),
        ],
        out_specs=pl.BlockSpec((None, tq, NSA_WIDTH), lambda b, i: (b, i, 0)),
        out_shape=jax.ShapeDtypeStruct((bsz, seq, NSA_WIDTH), BF16),
        scratch_shapes=[pltpu.VMEM((NSA_GROUPS, rows, swidth), F32), pltpu.VMEM((NSA_GROUPS, rows, swidth), BF16),
                        pltpu.VMEM((NSA_GROUPS, tq, swidth), F32),
                        pltpu.VMEM((NSA_GROUPS, rows, 1), F32), pltpu.VMEM((NSA_GROUPS, rows, 1), F32),
                        pltpu.VMEM((NSA_GROUPS, rows, NSA_HEAD_DIM), F32),
                        pltpu.VMEM((NSA_GROUPS, rows, NSA_HEAD_DIM), F32)],
        compiler_params=_cp("parallel", "arbitrary"),
        name="nsa_attention",
    )(q, kvc, kv, kv, kv, kv, small, jnp.asarray(ovt, BF16))


def _mla_prep_kernel(c_ref, kr_ref, pos_ref, qnw_ref, kvnw_ref, inv_ref, cq_ref, ckv_ref, k1_ref, k2_ref,
                     cos_ref, sin_ref):
    def rms(v, w):
        return v * lax.rsqrt(jnp.mean(v * v, -1, keepdims=True) + RMS_EPS) * w

    cq_ref[...] = rms(c_ref[:, 0:MLA_Q_RANK], qnw_ref[...]).astype(BF16)
    ckv_ref[...] = rms(c_ref[:, MLA_Q_RANK:MLA_Q_RANK + MLA_KV_RANK], kvnw_ref[...]).astype(BF16)
    ang = pos_ref[...].astype(F32) * inv_ref[...]
    cos = jnp.cos(ang)
    sin = jnp.sin(ang)
    half = MLA_ROPE // 2
    t1 = kr_ref[:, 0:half]
    t2 = kr_ref[:, half:MLA_ROPE]
    k1_ref[...] = (t1 * cos - t2 * sin).astype(BF16)
    k2_ref[...] = (t1 * sin + t2 * cos).astype(BF16)
    cos_ref[...] = cos
    sin_ref[...] = sin


def _mla_prep(c, kr, pos, q_norm_w, kv_norm_w, tm=512):
    m = c.shape[0]
    half = MLA_ROPE // 2
    inv = (1.0 / (ROPE_THETA ** (jnp.arange(0, MLA_ROPE, 2, dtype=F32) / MLA_ROPE)))[None, :]
    row = lambda w: pl.BlockSpec((tm, w), lambda i: (i, 0))
    vec = lambda w: pl.BlockSpec((1, w), lambda i: (0, 0))
    return pl.pallas_call(
        _mla_prep_kernel,
        grid=(m // tm,),
        in_specs=[row(MLA_Q_RANK + MLA_KV_RANK), row(LANES), row(1), vec(MLA_Q_RANK), vec(MLA_KV_RANK), vec(half)],
        out_specs=[row(MLA_Q_RANK), row(MLA_KV_RANK), row(half), row(half), row(half), row(half)],
        out_shape=[jax.ShapeDtypeStruct((m, MLA_Q_RANK), BF16), jax.ShapeDtypeStruct((m, MLA_KV_RANK), BF16),
                   jax.ShapeDtypeStruct((m, half), BF16), jax.ShapeDtypeStruct((m, half), BF16),
                   jax.ShapeDtypeStruct((m, half), F32), jax.ShapeDtypeStruct((m, half), F32)],
        compiler_params=_cp("parallel"),
        name="mla_prep",
    )(c, kr, pos, q_norm_w[None, :], kv_norm_w[None, :], inv)


def _q_rope_kernel(t_ref, cos_ref, sin_ref, e_ref, r1_ref, r2_ref):
    hw = MLA_HEADS * MLA_ROPE // 2
    cos = _dot_f32_by_exact(cos_ref[...], e_ref[...])
    sin = _dot_f32_by_exact(sin_ref[...], e_ref[...])
    t1 = t_ref[:, 0:hw]
    t2 = t_ref[:, hw:2 * hw]
    r1_ref[...] = (t1 * cos - t2 * sin).astype(BF16)
    r2_ref[...] = (t1 * sin + t2 * cos).astype(BF16)


def _q_rope(t, cos, sin, tm=512):
    m = t.shape[0]
    half = MLA_ROPE // 2
    hw = MLA_HEADS * half
    e = (np.arange(hw)[None, :] % half == np.arange(half)[:, None])
    row = lambda w: pl.BlockSpec((tm, w), lambda i: (i, 0))
    return pl.pallas_call(
        _q_rope_kernel,
        grid=(m // tm,),
        in_specs=[row(2 * hw), row(half), row(half), pl.BlockSpec((half, hw), lambda i: (0, 0))],
        out_specs=[row(hw), row(hw)],
        out_shape=[jax.ShapeDtypeStruct((m, hw), BF16)] * 2,
        compiler_params=_cp("parallel"),
        name="mla_q_rope",
    )(t, cos, sin, jnp.asarray(e, BF16))


def _mla_attn_kernel(qn_ref, qr_ref, kn_ref, kr_ref, v_ref, o_ref, s_ref):
    t = MLA_TILE
    seq = qn_ref.shape[0]
    c = (MLA_NOPE + MLA_ROPE) ** -0.5 * float(np.log2(np.e))
    causal = _iota((t, t), 1) <= _iota((t, t), 0)
    for qi in range(seq // t):
        rows = slice(qi * t, (qi + 1) * t)
        qn = qn_ref[rows, :]
        qr = qr_ref[rows, :]
        for kt in range(qi + 1):
            keys = slice(kt * t, (kt + 1) * t)
            s = _dot_nt(qn, kn_ref[keys, :]) + _dot_nt(qr, kr_ref[keys, :])
            s_ref[:, keys] = jnp.where(causal, s, NEG_INF) if kt == qi else s
        n = (qi + 1) * t
        s = s_ref[:, 0:n]
        p = jnp.exp2((s - jnp.max(s, -1, keepdims=True)) * c)
        o = _dot(p, v_ref[0:n, :]) / jnp.sum(p, -1, keepdims=True)
        o_ref[rows, :] = o.astype(o_ref.dtype)


def _mla_attention(qn, qr, kv, kr):
    bsz, seq, _ = qn.shape
    assert seq % MLA_TILE == 0
    return pl.pallas_call(
        _mla_attn_kernel,
        grid=(bsz, MLA_HEADS),
        in_specs=[
            pl.BlockSpec((None, seq, MLA_NOPE), lambda b, h: (b, 0, h)),
            pl.BlockSpec((None, None, seq, MLA_ROPE), lambda b, h: (b, h, 0, 0)),
            pl.BlockSpec((None, seq, MLA_NOPE), lambda b, h: (b, 0, 2 * h)),
            pl.BlockSpec((None, seq, MLA_ROPE), lambda b, h: (b, 0, 0)),
            pl.BlockSpec((None, seq, MLA_V), lambda b, h: (b, 0, 2 * h + 1)),
        ],
        out_specs=pl.BlockSpec((None, seq, MLA_V), lambda b, h: (b, 0, h)),
        out_shape=jax.ShapeDtypeStruct((bsz, seq, MLA_HEADS * MLA_V), BF16),
        scratch_shapes=[pltpu.VMEM((MLA_TILE, seq), F32)],
        compiler_params=_cp("parallel", "arbitrary"),
        name="mla_attention",
    )(qn, qr, kv, kr, kv)


def _hgrn_kernel(hq_ref, hf_ref, hi_ref, hg_ref, lb_ref, nw_ref, o_ref, st):
    C = HGRN_CHUNK

    @pl.when(pl.program_id(2) == 0)
    def _():
        st[...] = jnp.zeros_like(st)

    low = _iota((C, C), 1) <= _iota((C, C), 0)
    tri = jnp.where(low, 1.0, 0.0)
    for c in range(HGRN_TILE // C):
        rs = slice(c * C, (c + 1) * C)
        for j in range(HGRN_HEADS_PER_STEP):
            cs = slice(j * HGRN_KEY, (j + 1) * HGRN_KEY)
            lb = lb_ref[:, cs]
            hf = hf_ref[rs, cs]
            v = hi_ref[rs, cs]
            q = _silu(hq_ref[rs, cs])
            f = lb + (1.0 - lb) * jax.nn.sigmoid(hf)
            k = (1.0 - lb) * jax.nn.sigmoid(-hf)
            b = _dot_exact_by_f32(tri, jnp.log(f))
            b_last = b[C - 1:C, :]
            q_t = q * jnp.exp(b)
            k_t = k * jnp.exp(-b)
            k_end = k * jnp.exp(b_last - b)
            att = jnp.where(low, _dot_nt(q_t, k_t), 0.0)
            s_t = st[j]
            o = _dot(att, v) + _dot_nt(q_t, s_t)
            st[j] = s_t * jnp.exp(b_last) + _dot(v.T, k_end)
            o = o * lax.rsqrt(jnp.mean(o * o, -1, keepdims=True) + RMS_EPS) * nw_ref[...]
            o_ref[rs, cs] = (o * _silu(hg_ref[rs, cs])).astype(o_ref.dtype)


def _hgrn_mixer(hp, lb, norm_w):
    bsz, seq, _ = hp.shape
    t = HGRN_TILE
    hps = HGRN_HEADS_PER_STEP
    w = hps * HGRN_KEY
    nh = HGRN_HEADS // hps
    part = lambda p: pl.BlockSpec((None, t, w), lambda b, h, i: (b, i, p * nh + h))
    return pl.pallas_call(
        _hgrn_kernel,
        grid=(bsz, nh, seq // t),
        in_specs=[part(0), part(1), part(2), part(3),
                  pl.BlockSpec((None, 1, w), lambda b, h, i: (h, 0, 0)),
                  pl.BlockSpec((1, HGRN_VAL), lambda b, h, i: (0, 0))],
        out_specs=pl.BlockSpec((None, t, w), lambda b, h, i: (b, i, h)),
        out_shape=jax.ShapeDtypeStruct((bsz, seq, HGRN_WIDTH), BF16),
        scratch_shapes=[pltpu.VMEM((hps, HGRN_VAL, HGRN_KEY), F32)],
        compiler_params=_cp("parallel", "parallel", "arbitrary"),
        name="hgrn2_mixer",
    )(hp, hp, hp, hp, lb.reshape(nh, 1, w), norm_w[None, :])


def _router_kernel(x_ref, wt_ref, bias_ref, esel_ref, rank_ref, wsel_ref, cnt_ref, carry):
    tm = x_ref.shape[0]
    n_grp = N_EXPERT_GROUPS
    gsz = N_EXPERTS // n_grp
    ninf = -jnp.inf

    @pl.when(pl.program_id(0) == 0)
    def _():
        carry[...] = jnp.zeros_like(carry)

    x = x_ref[...]
    w = wt_ref[...]
    xh = x.astype(BF16)
    xl = (x - xh.astype(F32)).astype(BF16)
    wh = w.astype(BF16)
    wl = (w - wh.astype(F32)).astype(BF16)
    dnt = lambda a, b: lax.dot_general(a, b, (((1,), (1,)), ((), ())), preferred_element_type=F32)
    scores = jax.nn.sigmoid(dnt(wh, xh) + dnt(wh, xl) + dnt(wl, xh))
    biased = scores + bias_ref[...]
    sub = _iota((gsz, tm), 0).astype(F32)

    def top1(blk):
        mx = jnp.max(blk, 0, keepdims=True)
        return mx, jnp.min(jnp.where(blk == mx, sub, 1e9), 0, keepdims=True)

    blocks = [biased[g * gsz:(g + 1) * gsz, :] for g in range(n_grp)]
    grp = []
    for blk in blocks:
        m1, i1 = top1(blk)
        m2, _ = top1(jnp.where(sub == i1, ninf, blk))
        grp.append(m1 + m2)
    grp = jnp.concatenate(grp, axis=0)
    gsel = jnp.zeros((n_grp, tm), F32)
    for _ in range(TOP_GROUPS):
        _, first = top1(grp)
        hit = sub == first
        gsel = jnp.where(hit, 1.0, gsel)
        grp = jnp.where(hit, ninf, grp)

    cand = [jnp.where(gsel[g:g + 1, :] > 0.5, blocks[g], ninf) for g in range(n_grp)]
    eid = [sub + float(g * gsz) for g in range(n_grp)]
    sel = [jnp.zeros((gsz, tm), F32) for _ in range(n_grp)]
    firsts = []
    for _ in range(TOP_K):
        mm = cand[0]
        for g in range(1, n_grp):
            mm = jnp.maximum(mm, cand[g])
        mx = jnp.max(mm, 0, keepdims=True)
        fm = jnp.where(cand[0] == mx, eid[0], 1e9)
        for g in range(1, n_grp):
            fm = jnp.minimum(fm, jnp.where(cand[g] == mx, eid[g], 1e9))
        first = jnp.min(fm, 0, keepdims=True)
        firsts.append(first)
        for g in range(n_grp):
            hit = eid[g] == first
            sel[g] = jnp.where(hit, 1.0, sel[g])
            cand[g] = jnp.where(hit, ninf, cand[g])

    selm = jnp.concatenate(sel, axis=0)
    before = jnp.where(_iota((tm, tm), 0) < _iota((tm, tm), 1), 1.0, 0.0).astype(BF16)
    rank = jnp.dot(selm.astype(BF16), before, preferred_element_type=F32) + carry[...]
    carry[...] = carry[...] + jnp.sum(selm, -1, keepdims=True)
    cnt_ref[...] = carry[...]

    ws, rs = [], []
    for first in firsts:
        wacc = jnp.zeros((gsz, tm), F32)
        racc = jnp.zeros((gsz, tm), F32)
        for g in range(n_grp):
            hit = eid[g] == first
            wacc = jnp.where(hit, scores[g * gsz:(g + 1) * gsz, :], wacc)
            racc = jnp.where(hit, rank[g * gsz:(g + 1) * gsz, :], racc)
        ws.append(jnp.sum(wacc, 0, keepdims=True))
        rs.append(jnp.sum(racc, 0, keepdims=True))
    wsum = ws[0]
    for wk in ws[1:]:
        wsum = wsum + wk
    esel_ref[...] = jnp.concatenate(firsts, axis=0).astype(jnp.int32)
    rank_ref[...] = jnp.concatenate(rs, axis=0).astype(jnp.int32)
    wsel_ref[...] = jnp.concatenate(ws, axis=0) / wsum * ROUTED_SCALE


def _route(x, w_router, router_bias, tm=256):
    T, d = x.shape
    E = N_EXPERTS
    assert N_EXPERT_GROUPS == E // N_EXPERT_GROUPS == 8
    kt = pl.BlockSpec((TOP_K, tm), lambda i: (0, i))
    return pl.pallas_call(
        _router_kernel,
        grid=(T // tm,),
        in_specs=[pl.BlockSpec((tm, d), lambda i: (i, 0)), pl.BlockSpec((E, d), lambda i: (0, 0)),
                  pl.BlockSpec((E, 1), lambda i: (0, 0))],
        out_specs=[kt, kt, kt, pl.BlockSpec((E, 1), lambda i: (0, 0))],
        out_shape=[jax.ShapeDtypeStruct((TOP_K, T), jnp.int32), jax.ShapeDtypeStruct((TOP_K, T), jnp.int32),
                   jax.ShapeDtypeStruct((TOP_K, T), F32), jax.ShapeDtypeStruct((E, 1), F32)],
        scratch_shapes=[pltpu.VMEM((E, 1), F32)],
        compiler_params=_cp("arbitrary"),
        name="moe_router",
    )(x, w_router.T, router_bias.astype(F32)[:, None])


def _expert_kernel(be_ref, nv_ref, x_ref, wg_ref, wu_ref, wd_ref, o_ref, wg_b, wu_b, wd_b):
    i = pl.program_id(0)

    @pl.when((i == 0) | (be_ref[i] != be_ref[jnp.maximum(i - 1, 0)]))
    def _():
        wg_b[...] = wg_ref[...].astype(BF16)
        wu_b[...] = wu_ref[...].astype(BF16)
        wd_b[...] = wd_ref[...].astype(BF16)

    @pl.when(i < nv_ref[0])
    def _():
        half = x_ref.shape[1]
        hi, lo = _unpack_bf16_pairs(x_ref[...])
        xh = hi.astype(BF16)
        xl = lo.astype(BF16)

        def xw(w_b):
            return (jnp.dot(xh, w_b[0:half, :], preferred_element_type=F32)
                    + jnp.dot(xl, w_b[half:, :], preferred_element_type=F32))

        h = _silu(xw(wg_b)) * xw(wu_b)
        o_ref[...] = _pack_bf16_pairs(jnp.dot(h.astype(BF16), wd_b[...], preferred_element_type=F32))

    @pl.when(i >= nv_ref[0])
    def _():
        o_ref[...] = jnp.zeros_like(o_ref)


def _pack_bf16_pairs(y):
    half = y.shape[1] // 2
    bits = lambda v: lax.bitcast_convert_type(v.astype(BF16).astype(F32), jnp.uint32)
    return bits(y[:, :half]) | (bits(y[:, half:]) >> 16)


def _unpack_bf16_pairs(u):
    hi = lax.bitcast_convert_type(u & jnp.uint32(0xFFFF0000), F32)
    lo = lax.bitcast_convert_type(u << 16, F32)
    return hi, lo


def _expert_ffn(x_rows, blk_e, n_valid, w_gate, w_up, w_down, layer, mb):
    p = x_rows.shape[0]
    d = 2 * x_rows.shape[1]
    ff = w_gate.shape[3]
    xmap = lambda i, be, nv: (jnp.minimum(i, nv[0] - 1), 0)
    wmap = lambda i, be, nv: (layer, be[i], 0, 0)
    grid_spec = pltpu.PrefetchScalarGridSpec(
        num_scalar_prefetch=2,
        grid=(p // mb,),
        in_specs=[
            pl.BlockSpec((mb, d // 2), xmap),
            pl.BlockSpec((None, None, d, ff), wmap), pl.BlockSpec((None, None, d, ff), wmap),
            pl.BlockSpec((None, None, ff, d), wmap),
        ],
        out_specs=pl.BlockSpec((mb, d // 2), lambda i, be, nv: (i, 0)),
        scratch_shapes=[pltpu.VMEM((d, ff), BF16), pltpu.VMEM((d, ff), BF16), pltpu.VMEM((ff, d), BF16)],
    )
    return pl.pallas_call(
        _expert_kernel,
        grid_spec=grid_spec,
        out_shape=jax.ShapeDtypeStruct((p, d // 2), jnp.uint32),
        compiler_params=pltpu.CompilerParams(dimension_semantics=("arbitrary",), vmem_limit_bytes=EXPERT_VMEM_LIMIT),
        name="moe_expert_ffn",
    )(blk_e, n_valid, x_rows, w_gate, w_up, w_down)


def _moe_ffn(x, x_pk, layer, w_router, router_bias, w_gate, w_up, w_down, ws_gate, ws_up, ws_down):
    T, d = x.shape
    E = N_EXPERTS
    M = MOE_ROW_BLOCK
    esel, rank, wsel, cnt = _route(x, w_router, router_bias)
    counts = cnt[:, 0].astype(jnp.int32)
    pcounts = (counts + M - 1) // M * M
    pends = jnp.cumsum(pcounts)
    pstarts = pends - pcounts
    NB = -(-(T * TOP_K) // M) + E
    blk_e = jnp.minimum(jnp.sum(pends[None, :] <= (jnp.arange(NB, dtype=jnp.int32) * M)[:, None], axis=1),
                        E - 1).astype(jnp.int32)
    n_valid = (pends[-1] // M).astype(jnp.int32).reshape(1)
    slot = rank + jnp.sum(jnp.where(esel[:, :, None] == jnp.arange(E, dtype=jnp.int32), pstarts, 0), axis=-1)
    slot = slot.reshape(TOP_K * T)
    row_tok = jnp.zeros((NB * M,), jnp.int32).at[slot].set(
        jnp.tile(jnp.arange(T, dtype=jnp.int32), TOP_K), unique_indices=True)
    y_rows = _expert_ffn(x_pk[row_tok], blk_e, n_valid, w_gate, w_up, w_down, layer, M)
    y_k = y_rows[slot].reshape(TOP_K, T, d // 2)

    depth = ws_gate.shape[0]
    ms = SHARED_ROW_BLOCK
    shared = _expert_ffn(x_pk, jnp.zeros((T // ms,), jnp.int32), jnp.full((1,), T // ms, jnp.int32),
                         ws_gate.reshape(depth, 1, d, -1), ws_up.reshape(depth, 1, d, -1),
                         ws_down.reshape(depth, 1, -1, d), layer, ms)
    return shared, y_k, wsel.T


def _mixer_ab(x_bf, bsz, seq, w_in, conv_w, conv_b, dt_bias, a_log, d_skip, norm_w, cmp_pos, cmp_w, w_out):
    T = bsz * seq
    o_zx = SSM_INNER + SSM_CONV_DIM
    o_dt = o_zx + SSM_HEADS
    o_q = o_dt + NSA_WIDTH
    o_kv = o_q + 6 * NSA_KV
    w_bf = w_in.astype(BF16)
    zx = _matmul(x_bf, w_bf[:, :o_zx], F32)
    w_small = _pad_cols(jnp.concatenate([w_bf[:, o_zx:o_dt], w_bf[:, o_kv:]], axis=1), LANES)
    small = _matmul(x_bf, w_small, F32)
    q = _matmul(x_bf, w_bf[:, o_dt:o_q], BF16)
    kv = _matmul(x_bf, w_bf[:, o_q:o_kv], BF16)

    zx3 = zx.reshape(bsz, seq, o_zx)
    small3 = small.reshape(bsz, seq, LANES)
    dt_t = jnp.swapaxes(small3[:, :, :SSM_HEADS], 1, 2)
    y_a = _ssd_mixer(zx3, small3, dt_t, conv_w, conv_b, dt_bias, a_log, d_skip, norm_w)

    kv3 = kv.reshape(bsz, seq, 6 * NSA_KV)
    ncp = seq // NSA_CMP_STRIDE
    strips = kv3[:, :, :2 * NSA_KV].reshape(bsz, ncp, NSA_CMP_STRIDE, 2 * NSA_GROUPS, NSA_HEAD_DIM)
    strips = strips.transpose(0, 3, 1, 2, 4).reshape(bsz, 2 * NSA_GROUPS, ncp, NSA_CMP_STRIDE * NSA_HEAD_DIM)
    kvc = _nsa_compress(strips, cmp_w, cmp_pos)
    y_b = _nsa_mixer(q.reshape(bsz, seq, NSA_WIDTH), kv3, small3, kvc)

    return _matmul_cat(y_a.reshape(T, SSM_INNER), y_b.reshape(T, NSA_WIDTH), w_out.astype(BF16), F32)


def _mixer_cd(x_bf, bsz, seq, positions, w_in, q_norm_w, w_uq, kv_norm_w, w_ukv, lb, hgrn_norm_w, w_out):
    T = bsz * seq
    o_c = MLA_Q_RANK + MLA_KV_RANK
    o_kr = o_c + MLA_ROPE
    half = MLA_ROPE // 2
    w_bf = w_in.astype(BF16)
    c = _matmul(x_bf, w_bf[:, :o_c], F32)
    kr = _matmul(x_bf, _pad_cols(w_bf[:, o_c:o_kr], LANES), F32)
    hp = _matmul(x_bf, w_bf[:, o_kr:], F32)

    cqn, ckvn, k1, k2, cos, sin = _mla_prep(c, kr, positions.reshape(T, 1), q_norm_w, kv_norm_w)
    wq = w_uq.astype(BF16).reshape(MLA_Q_RANK, MLA_HEADS, MLA_NOPE + MLA_ROPE)
    w_nope = wq[:, :, :MLA_NOPE].reshape(MLA_Q_RANK, MLA_HEADS * MLA_NOPE)
    w_rope = jnp.concatenate([wq[:, :, MLA_NOPE:MLA_NOPE + half].reshape(MLA_Q_RANK, MLA_HEADS * half),
                              wq[:, :, MLA_NOPE + half:].reshape(MLA_Q_RANK, MLA_HEADS * half)], axis=1)
    qn = _matmul(cqn, w_nope, BF16)
    r1, r2 = _q_rope(_matmul(cqn, w_rope, F32), cos, sin)
    qr = jnp.concatenate([r1.reshape(bsz, seq, MLA_HEADS, half), r2.reshape(bsz, seq, MLA_HEADS, half)], axis=-1)
    qr = qr.transpose(0, 2, 1, 3)
    kvu = _matmul(ckvn, w_ukv.astype(BF16), BF16)
    y_c = _mla_attention(qn.reshape(bsz, seq, -1), qr, kvu.reshape(bsz, seq, -1),
                         jnp.concatenate([k1, k2], axis=1).reshape(bsz, seq, MLA_ROPE))
    y_d = _hgrn_mixer(hp.reshape(bsz, seq, -1), lb, hgrn_norm_w)
    return _matmul_cat(y_c.reshape(T, -1), y_d.reshape(T, -1), w_out.astype(BF16), F32)


def kernel(x, positions, ab_w_in, ssm_conv_w, ssm_conv_b, ssm_dt_bias, ssm_a_log, ssm_d, ssm_norm_w, nsa_cmp_pos, nsa_cmp_w, ab_w_out, cd_w_in, mla_q_norm_w, mla_w_uq, mla_kv_norm_w, mla_w_ukv, hgrn_lb_logits, hgrn_norm_w, cd_w_out, ln_g, ln_b, moe_w_router, moe_router_bias, moe_w_gate, moe_w_up, moe_w_down, moe_shared_w_gate, moe_shared_w_up, moe_shared_w_down):
    bsz, seq, d = x.shape
    T = bsz * seq
    lb_all = jnp.cumsum(jax.nn.softmax(hgrn_lb_logits.astype(F32), axis=0), axis=0)
    lb_all = lb_all - lb_all[0]
    xf = x.reshape(T, d)
    x_bf = xf.astype(BF16)
    for l in range(DEPTH):
        i = l // 2
        if l % 2 == 0:
            mix = _mixer_ab(x_bf, bsz, seq, ab_w_in[i], ssm_conv_w[i], ssm_conv_b[i], ssm_dt_bias[i], ssm_a_log[i],
                            ssm_d[i], ssm_norm_w[i], nsa_cmp_pos[i], nsa_cmp_w[i], ab_w_out[i])
        else:
            mix = _mixer_cd(x_bf, bsz, seq, positions, cd_w_in[i], mla_q_norm_w[i], mla_w_uq[i], mla_kv_norm_w[i],
                            mla_w_ukv[i], lb_all[l], hgrn_norm_w[i], cd_w_out[i])
        xf, x_pk = _deepnorm_ln(xf, mix, ln_g[l, 0], ln_b[l, 0])
        shared, y_k, w_k = _moe_ffn(xf, x_pk, l, moe_w_router[l], moe_router_bias[l], moe_w_gate, moe_w_up,
                                    moe_w_down, moe_shared_w_gate, moe_shared_w_up, moe_shared_w_down)
        xf, x_bf = _deepnorm_ln_combine(xf, shared, y_k, w_k, ln_g[l, 1], ln_b[l, 1])
    return xf.reshape(bsz, seq, d)
```

```python
import functools

import numpy as np
import jax
import jax.numpy as jnp
from jax import lax
from jax.experimental import pallas as pl
from jax.experimental.pallas import tpu as pltpu

F32 = jnp.float32
BF16 = jnp.bfloat16

D_MODEL = 2048
DEPTH = 2
DN_ALPHA = float((2 * DEPTH) ** 0.25)
LN_EPS = 1e-5
RMS_EPS = 1e-6
NEG_INF = -1e30

SSM_HEAD_DIM = 64
SSM_INNER = D_MODEL
SSM_HEADS = SSM_INNER // SSM_HEAD_DIM
SSM_STATE = 128
SSM_GROUPS = 4
SSM_CONV = 4
SSM_CHUNK = 128
SSM_BC = SSM_GROUPS * SSM_STATE
SSM_CONV_DIM = SSM_INNER + 2 * SSM_BC
SSM_GROUP_W = SSM_INNER // SSM_GROUPS

NSA_HEADS = 16
NSA_GROUPS = 2
NSA_HG = NSA_HEADS // NSA_GROUPS
NSA_HEAD_DIM = D_MODEL // NSA_HEADS
NSA_WIDTH = NSA_HEADS * NSA_HEAD_DIM
NSA_KV = NSA_GROUPS * NSA_HEAD_DIM
NSA_CMP_BLOCK = 32
NSA_CMP_STRIDE = 16
NSA_SEL_BLOCK = 64
NSA_TOP_N = 8
NSA_WINDOW = 512
NSA_FORCED = 1e6
NSA_Q_TILE = 128
NSA_SEL_K_TILE = 512

MLA_HEADS = 16
MLA_Q_RANK = 512
MLA_KV_RANK = 512
MLA_NOPE = 128
MLA_ROPE = 64
MLA_V = 128
ROPE_THETA = 10000.0
MLA_TILE = 512

HGRN_HEADS = 16
HGRN_KEY = 128
HGRN_VAL = 128
HGRN_CHUNK = 64
HGRN_FDIM = HGRN_HEADS * HGRN_KEY
HGRN_WIDTH = HGRN_HEADS * HGRN_VAL
HGRN_TILE = 512
HGRN_HEADS_PER_STEP = 2

N_EXPERTS = 64
TOP_K = 8
N_EXPERT_GROUPS = 8
TOP_GROUPS = 4
EXPERT_FF = 512
ROUTED_SCALE = 2.5
MOE_ROW_BLOCK = 512
SHARED_ROW_BLOCK = 512

LANES = 128
VMEM_LIMIT = 48 * 1024 * 1024
EXPERT_VMEM_LIMIT = 56 * 1024 * 1024


def _cp(*sem):
    return pltpu.CompilerParams(dimension_semantics=sem, vmem_limit_bytes=VMEM_LIMIT)


def _dot(a, b):
    return jnp.dot(a.astype(BF16), b.astype(BF16), preferred_element_type=F32)


def _dot_nt(a, b):
    return lax.dot_general(a.astype(BF16), b.astype(BF16), (((1,), (1,)), ((), ())),
                           preferred_element_type=F32)


def _split3(a):
    a1 = a.astype(BF16)
    r1 = a - a1.astype(F32)
    a2 = r1.astype(BF16)
    r2 = r1 - a2.astype(F32)
    return a1, a2, r2.astype(BF16)


def _dot_f32_by_exact(a, t):
    t = t.astype(BF16)
    return sum(jnp.dot(p, t, preferred_element_type=F32) for p in _split3(a))


def _dot_exact_by_f32(t, a):
    t = t.astype(BF16)
    return sum(jnp.dot(t, p, preferred_element_type=F32) for p in _split3(a))


def _silu(x):
    return x * jax.nn.sigmoid(x)


def _softplus(x):
    return jnp.maximum(x, 0.0) + jnp.log(1.0 + jnp.exp(-jnp.abs(x)))


def _iota(shape, dim):
    return lax.broadcasted_iota(jnp.int32, shape, dim)


def _mm_kernel(x_ref, w_ref, o_ref):
    o_ref[...] = jnp.dot(x_ref[...], w_ref[...], preferred_element_type=F32).astype(o_ref.dtype)


def _mm2_kernel(xa_ref, xb_ref, w_ref, o_ref):
    ka = xa_ref.shape[1]
    acc = jnp.dot(xa_ref[...], w_ref[0:ka, :], preferred_element_type=F32)
    acc = acc + jnp.dot(xb_ref[...], w_ref[ka:, :], preferred_element_type=F32)
    o_ref[...] = acc.astype(o_ref.dtype)


def _pick_tile(n, pref):
    if n <= pref:
        return n
    t = pref - pref % LANES
    while n % t:
        t -= LANES
    return t


def _matmul(x, w, out_dtype, tm=2048, tn=512):
    m, k = x.shape
    n = w.shape[1]
    tm = _pick_tile(m, tm)
    tn = _pick_tile(n, tn)
    return pl.pallas_call(
        _mm_kernel,
        grid=(m // tm, n // tn),
        in_specs=[pl.BlockSpec((tm, k), lambda i, j: (i, 0)), pl.BlockSpec((k, tn), lambda i, j: (0, j))],
        out_specs=pl.BlockSpec((tm, tn), lambda i, j: (i, j)),
        out_shape=jax.ShapeDtypeStruct((m, n), out_dtype),
        compiler_params=_cp("parallel", "arbitrary"),
        name="matmul",
    )(x, w)


def _matmul_cat(xa, xb, w, out_dtype, tm=1024, tn=1024):
    m, ka = xa.shape
    kb = xb.shape[1]
    n = w.shape[1]
    tm = _pick_tile(m, tm)
    tn = _pick_tile(n, tn)
    return pl.pallas_call(
        _mm2_kernel,
        grid=(m // tm, n // tn),
        in_specs=[pl.BlockSpec((tm, ka), lambda i, j: (i, 0)), pl.BlockSpec((tm, kb), lambda i, j: (i, 0)),
                  pl.BlockSpec((ka + kb, tn), lambda i, j: (0, j))],
        out_specs=pl.BlockSpec((tm, tn), lambda i, j: (i, j)),
        out_shape=jax.ShapeDtypeStruct((m, n), out_dtype),
        compiler_params=_cp("parallel", "arbitrary"),
        name="matmul_cat",
    )(xa, xb, w)


def _pad_cols(w, mult):
    n = w.shape[1]
    pad = (-n) % mult
    return jnp.pad(w, ((0, 0), (0, pad))) if pad else w


def _ln_combine_kernel(x_ref, sh_ref, y_ref, w_ref, g_ref, b_ref, o_ref, obf_ref):
    hi, lo = _unpack_bf16_pairs(sh_ref[...])
    w = w_ref[...]
    for k in range(TOP_K):
        yh, yl = _unpack_bf16_pairs(y_ref[k])
        hi = hi + w[:, k:k + 1] * yh
        lo = lo + w[:, k:k + 1] * yl
    v = DN_ALPHA * x_ref[...] + jnp.concatenate([hi, lo], axis=1)
    mu = jnp.mean(v, -1, keepdims=True)
    d = v - mu
    var = jnp.mean(d * d, -1, keepdims=True)
    y = d * lax.rsqrt(var + LN_EPS) * g_ref[...] + b_ref[...]
    o_ref[...] = y
    obf_ref[...] = y.astype(BF16)


def _deepnorm_ln_combine(x, shared, y_k, w_k, g, b, tm=128):
    m, d = x.shape
    row = pl.BlockSpec((tm, d), lambda i: (i, 0))
    vec = pl.BlockSpec((1, d), lambda i: (0, 0))
    return pl.pallas_call(
        _ln_combine_kernel,
        grid=(m // tm,),
        in_specs=[row, pl.BlockSpec((tm, d // 2), lambda i: (i, 0)),
                  pl.BlockSpec((TOP_K, tm, d // 2), lambda i: (0, i, 0)),
                  pl.BlockSpec((tm, TOP_K), lambda i: (i, 0)), vec, vec],
        out_specs=[row, row],
        out_shape=[jax.ShapeDtypeStruct((m, d), F32), jax.ShapeDtypeStruct((m, d), BF16)],
        compiler_params=_cp("parallel"),
        name="deepnorm_ln_combine",
    )(x, shared, y_k, w_k, g.reshape(1, d), b.reshape(1, d))


def _ln_kernel(x_ref, mix_ref, g_ref, b_ref, o_ref, opk_ref):
    v = DN_ALPHA * x_ref[...] + mix_ref[...]
    mu = jnp.mean(v, -1, keepdims=True)
    d = v - mu
    var = jnp.mean(d * d, -1, keepdims=True)
    y = d * lax.rsqrt(var + LN_EPS) * g_ref[...] + b_ref[...]
    o_ref[...] = y
    opk_ref[...] = _pack_bf16_pairs(y)


def _deepnorm_ln(x, mix, g, b, tm=256):
    m, d = x.shape
    row = pl.BlockSpec((tm, d), lambda i: (i, 0))
    vec = pl.BlockSpec((1, d), lambda i: (0, 0))
    return pl.pallas_call(
        _ln_kernel,
        grid=(m // tm,),
        in_specs=[row, row, vec, vec],
        out_specs=[row, pl.BlockSpec((tm, d // 2), lambda i: (i, 0))],
        out_shape=[jax.ShapeDtypeStruct((m, d), F32), jax.ShapeDtypeStruct((m, d // 2), jnp.uint32)],
        compiler_params=_cp("parallel"),
        name="deepnorm_ln",
    )(x, mix, g.reshape(1, d), b.reshape(1, d))


def _ssd_kernel(z_ref, xs_ref, bc_ref, dt_ref, dtt_ref, cwx_ref, cbx_ref, cwb_ref, cbb_ref, dtb_ref, dtbt_ref,
                alog_ref, alogt_ref, dexp_ref, nw_ref, e_ref, o_ref, extx, extb, st, y_acc):
    L = SSM_CHUNK
    halo = 8

    @pl.when(pl.program_id(1) == 0)
    def _():
        extx[0:halo, :] = jnp.zeros((halo, SSM_INNER), F32)
        extb[0:halo, :] = jnp.zeros((halo, 2 * SSM_BC), F32)
        st[...] = jnp.zeros_like(st)

    extx[halo:halo + L, :] = xs_ref[...]
    extb[halo:halo + L, :] = bc_ref[...]

    def conv(ext, w_ref, b_ref):
        acc = b_ref[...]
        for k in range(SSM_CONV):
            acc = acc + ext[pl.ds(halo - (SSM_CONV - 1) + k, L), :] * w_ref[k:k + 1, :]
        return acc

    xs = _silu(conv(extx, cwx_ref, cbx_ref))
    bc = _silu(conv(extb, cwb_ref, cbb_ref))
    extx[0:halo, :] = xs_ref[L - halo:L, :]
    extb[0:halo, :] = bc_ref[L - halo:L, :]

    li = _iota((L, L), 0)
    si = _iota((L, L), 1)
    low = si <= li
    dt = _softplus(dt_ref[:, 0:SSM_HEADS] + dtb_ref[...])
    adt = dt * (-jnp.exp(alog_ref[...]))
    a_cs = _dot_exact_by_f32(jnp.where(low, 1.0, 0.0), adt)
    dtt = _softplus(dtt_ref[...] + dtbt_ref[...])
    adtt = dtt * (-jnp.exp(alogt_ref[...]))
    a_cst = _dot_f32_by_exact(adtt, jnp.where(li <= si, 1.0, 0.0))

    e = e_ref[...]
    dt_e = _dot_f32_by_exact(dt, e)
    acs_e = _dot_f32_by_exact(a_cs, e)
    a_last = acs_e[L - 1:L, :]
    x_dt = xs * dt_e
    xw = (x_dt * jnp.exp(a_last - acs_e)).astype(BF16)
    ea = jnp.exp(acs_e)
    chunk_decay = jnp.exp(a_last)
    x_dt_b = x_dt.astype(BF16)
    lane = _iota((L, 2 * SSM_HEAD_DIM), 1)

    for g in range(SSM_GROUPS):
        gc = slice(g * SSM_GROUP_W, (g + 1) * SSM_GROUP_W)
        bm = bc[:, g * SSM_STATE:(g + 1) * SSM_STATE]
        cm = bc[:, SSM_BC + g * SSM_STATE:SSM_BC + (g + 1) * SSM_STATE]
        cb = _dot_nt(cm, bm)
        st_g = st[:, gc]
        y_acc[:, gc] = _dot(cm, st_g) * ea[:, gc]
        for pr in range(SSM_GROUP_W // (2 * SSM_HEAD_DIM)):
            h0 = (g * SSM_GROUP_W) // SSM_HEAD_DIM + 2 * pr
            cols = slice(h0 * SSM_HEAD_DIM, (h0 + 2) * SSM_HEAD_DIM)
            xr = x_dt_b[:, cols]
            res = []
            for h in (h0, h0 + 1):
                seg = a_cs[:, h:h + 1] - a_cst[h:h + 1, :]
                res.append(_dot(cb * jnp.where(low, jnp.exp(seg), 0.0), xr))
            y_acc[:, cols] += jnp.where(lane < SSM_HEAD_DIM, res[0], res[1])
        st[:, gc] = st_g * chunk_decay[:, gc] + _dot(bm.T, xw[:, gc])

    y = y_acc[...] + xs * dexp_ref[...]
    y = y * _silu(z_ref[...])
    var = jnp.mean(y * y, -1, keepdims=True)
    o_ref[...] = (y * lax.rsqrt(var + RMS_EPS) * nw_ref[...]).astype(o_ref.dtype)


def _ssd_mixer(zx, small, dt_t, conv_w, conv_b, dt_bias, a_log, d_skip, norm_w):
    bsz, seq, _ = zx.shape
    L = SSM_CHUNK
    nblk = SSM_INNER // (2 * SSM_BC)
    e = (np.arange(SSM_INNER)[None, :] // SSM_HEAD_DIM == np.arange(SSM_HEADS)[:, None])
    full = lambda shape: pl.BlockSpec(shape, lambda b, c: (0,) * len(shape))
    return pl.pallas_call(
        _ssd_kernel,
        grid=(bsz, seq // L),
        in_specs=[
            pl.BlockSpec((None, L, SSM_INNER), lambda b, c: (b, c, 0)),
            pl.BlockSpec((None, L, SSM_INNER), lambda b, c: (b, c, 1)),
            pl.BlockSpec((None, L, 2 * SSM_BC), lambda b, c: (b, c, 2 * nblk)),
            pl.BlockSpec((None, L, LANES), lambda b, c: (b, c, 0)),
            pl.BlockSpec((None, SSM_HEADS, L), lambda b, c: (b, 0, c)),
            full((SSM_CONV, SSM_INNER)), full((1, SSM_INNER)),
            full((SSM_CONV, 2 * SSM_BC)), full((1, 2 * SSM_BC)),
            full((1, SSM_HEADS)), full((SSM_HEADS, 1)), full((1, SSM_HEADS)), full((SSM_HEADS, 1)),
            full((1, SSM_INNER)), full((1, SSM_INNER)), full((SSM_HEADS, SSM_INNER)),
        ],
        out_specs=pl.BlockSpec((None, L, SSM_INNER), lambda b, c: (b, c, 0)),
        out_shape=jax.ShapeDtypeStruct((bsz, seq, SSM_INNER), BF16),
        scratch_shapes=[
            pltpu.VMEM((L + 8, SSM_INNER), F32), pltpu.VMEM((L + 8, 2 * SSM_BC), F32),
            pltpu.VMEM((SSM_STATE, SSM_INNER), F32), pltpu.VMEM((L, SSM_INNER), F32),
        ],
        compiler_params=_cp("arbitrary", "arbitrary"),
        name="ssd_mixer",
    )(zx, zx, zx, small, dt_t,
      conv_w[:, :SSM_INNER], conv_b[None, :SSM_INNER], conv_w[:, SSM_INNER:], conv_b[None, SSM_INNER:],
      dt_bias[None, :], dt_bias[:, None], a_log[None, :], a_log[:, None],
      jnp.repeat(d_skip, SSM_HEAD_DIM)[None, :], norm_w[None, :], jnp.asarray(e, BF16))


def _nsa_compress_kernel(r_ref, w_ref, pos_ref, o_ref):
    ncp = r_ref.shape[0]
    half = NSA_CMP_STRIDE * NSA_HEAD_DIM
    r = r_ref[...].astype(F32)
    top = _dot(r + pos_ref[0:1, :], w_ref[0:half, :])
    bot = _dot(r + pos_ref[1:2, :], w_ref[half:2 * half, :])
    kc = top + pltpu.roll(bot, ncp - 1, axis=0)
    o_ref[...] = jnp.where(_iota(kc.shape, 0) < ncp - 1, kc, 0.0).astype(o_ref.dtype)


def _nsa_compress(strips, cmp_w, cmp_pos):
    bsz, _, ncp, width = strips.shape
    return pl.pallas_call(
        _nsa_compress_kernel,
        grid=(bsz, 2 * NSA_GROUPS),
        in_specs=[
            pl.BlockSpec((None, None, ncp, width), lambda b, j: (b, j, 0, 0)),
            pl.BlockSpec((None, 2 * width, NSA_HEAD_DIM), lambda b, j: (j // NSA_GROUPS, 0, 0)),
            pl.BlockSpec((None, 2, width), lambda b, j: (j // NSA_GROUPS, 0, 0)),
        ],
        out_specs=pl.BlockSpec((None, None, ncp, NSA_HEAD_DIM), lambda b, j: (b, j, 0, 0)),
        out_shape=jax.ShapeDtypeStruct((bsz, 2 * NSA_GROUPS, ncp, NSA_HEAD_DIM), BF16),
        compiler_params=_cp("parallel", "arbitrary"),
        name="nsa_compress",
    )(strips, cmp_w.astype(BF16), cmp_pos.reshape(2, 2, width))


def _nsa_kernel(q_ref, kvc_ref, ksel_ref, vsel_ref, kwin_ref, vwin_ref, gate_ref, ovt_ref, o_ref,
                s_ref, p_ref, bias_ref, m_ref, l_ref, acc_ref, ocmp_ref, *, seq):
    tq = NSA_Q_TILE
    tk = NSA_SEL_K_TILE
    dh = NSA_HEAD_DIM
    ncp = seq // NSA_CMP_STRIDE
    n_cmp = ncp - 1
    n_sel = seq // NSA_SEL_BLOCK
    top_n = min(NSA_TOP_N, n_sel)
    wlen = NSA_WINDOW + tq
    c = dh ** -0.5 * float(np.log2(np.e))
    q0 = pl.program_id(1) * tq
    head_rows = [slice(h * tq, (h + 1) * tq) for h in range(NSA_HG)]

    gates = jax.nn.sigmoid(gate_ref[...])
    has_cmp = q0 + _iota((tq, 1), 0) >= NSA_CMP_BLOCK - 1
    ci = _iota((tq, ncp), 1)
    cmp_ok = (ci * NSA_CMP_STRIDE + (NSA_CMP_BLOCK - 1) <= q0 + _iota((tq, ncp), 0)) & (ci < n_cmp)
    cmp_bias = jnp.where(cmp_ok, 0.0, NEG_INF)

    groups = range(NSA_GROUPS)
    gcs = [slice(g * dh, (g + 1) * dh) for g in groups]
    qss = [jnp.concatenate([q_ref[:, (g * NSA_HG + h) * dh:(g * NSA_HG + h + 1) * dh] for h in range(NSA_HG)],
                           axis=0) for g in groups]
    selbs = []
    for g in groups:
        s_ref[g, :, 0:ncp] = _dot_nt(qss[g], kvc_ref[g])
        psum = jnp.zeros((tq, ncp), F32)
        for rs in head_rows:
            sc = s_ref[g, rs, 0:ncp] + cmp_bias
            pe = jnp.exp2((sc - jnp.max(sc, -1, keepdims=True)) * c)
            p = jnp.where(has_cmp, pe / jnp.sum(pe, -1, keepdims=True), 0.0)
            psum = psum + p
            p_ref[g, rs, 0:ncp] = p.astype(BF16)
        ocmp_ref[g] = _dot(p_ref[g, :, 0:ncp], kvc_ref[NSA_GROUPS + g])

        ovt = ovt_ref[...]
        imp = sum(lax.dot_general(ovt, part, (((1,), (1,)), ((), ())), preferred_element_type=F32)
                  for part in _split3(psum))
        j = _iota((LANES, tq), 0)
        tok = q0 + _iota((LANES, tq), 1)
        cur = tok >> 6
        forced = (j == 0) | (j == cur) | (j == cur - 1)
        imp = jnp.where(forced, NSA_FORCED, jnp.where(j * NSA_SEL_BLOCK <= tok, imp, -1.0))
        imp = jnp.where(j < n_sel, imp, -2.0)
        jf = j.astype(F32)
        sel = jnp.zeros((LANES, tq), F32)
        for _ in range(top_n):
            mx = jnp.max(imp, 0, keepdims=True)
            first = jnp.min(jnp.where(imp == mx, jf, 1e9), 0, keepdims=True)
            hit = jf == first
            sel = jnp.where(hit, 1.0, sel)
            imp = jnp.where(hit, -3.0, imp)
        selbs.append(sel.T.astype(BF16))

    m_ref[...] = jnp.full(m_ref.shape, NEG_INF, F32)
    l_ref[...] = jnp.zeros(l_ref.shape, F32)
    acc_ref[...] = jnp.zeros(acc_ref.shape, F32)

    def sel_step(kt, carry):
        k0 = pl.multiple_of(kt * tk, tk)
        blk_of_key = (k0 + _iota((LANES, tk), 1)) >> 6
        expand = jnp.where(blk_of_key == _iota((LANES, tk), 0), 1.0, 0.0).astype(BF16)
        causal = k0 + _iota((tq, tk), 1) <= q0 + _iota((tq, tk), 0)
        for g in groups:
            s_ref[g, :, 0:tk] = _dot_nt(qss[g], ksel_ref[pl.ds(k0, tk), gcs[g]])
            picked = jnp.dot(selbs[g], expand, preferred_element_type=F32)
            bias_ref[g, :, 0:tk] = jnp.where((picked > 0.5) & causal, 0.0, NEG_INF)
            for rs in head_rows:
                sc = s_ref[g, rs, 0:tk] + bias_ref[g, :, 0:tk]
                m_old = m_ref[g, rs, :]
                m_new = jnp.maximum(m_old, jnp.max(sc, -1, keepdims=True))
                alpha = jnp.exp2((m_old - m_new) * c)
                pe = jnp.exp2((sc - m_new) * c)
                l_ref[g, rs, :] = alpha * l_ref[g, rs, :] + jnp.sum(pe, -1, keepdims=True)
                acc_ref[g, rs, :] = alpha * acc_ref[g, rs, :]
                m_ref[g, rs, :] = m_new
                p_ref[g, rs, 0:tk] = pe.astype(BF16)
            acc_ref[g] += _dot(p_ref[g, :, 0:tk], vsel_ref[pl.ds(k0, tk), gcs[g]])
        return carry

    lax.fori_loop(0, (q0 + tq + tk - 1) // tk, sel_step, 0)

    ws = pl.multiple_of(jnp.maximum(q0 - NSA_WINDOW, 0), tq)
    kpos = ws + _iota((tq, wlen), 1)
    tpos = q0 + _iota((tq, wlen), 0)
    win_bias = jnp.where((kpos <= tpos) & (kpos > tpos - NSA_WINDOW), 0.0, NEG_INF)
    for g in groups:
        o_sel = acc_ref[g] / l_ref[g]
        s_ref[g, :, 0:wlen] = _dot_nt(qss[g], kwin_ref[pl.ds(ws, wlen), gcs[g]])
        for rs in head_rows:
            sc = s_ref[g, rs, 0:wlen] + win_bias
            pe = jnp.exp2((sc - jnp.max(sc, -1, keepdims=True)) * c)
            l_ref[g, rs, :] = jnp.sum(pe, -1, keepdims=True)
            p_ref[g, rs, 0:wlen] = pe.astype(BF16)
        o_win = _dot(p_ref[g, :, 0:wlen], vwin_ref[pl.ds(ws, wlen), gcs[g]]) / l_ref[g]
        for h, rs in enumerate(head_rows):
            hh = g * NSA_HG + h
            c0 = SSM_HEADS + 3 * hh
            o = (gates[:, c0:c0 + 1] * ocmp_ref[g, rs, :] + gates[:, c0 + 1:c0 + 2] * o_sel[rs]
                 + gates[:, c0 + 2:c0 + 3] * o_win[rs])
            o_ref[:, hh * dh:(hh + 1) * dh] = o.astype(o_ref.dtype)


def _nsa_mixer(q, kv, small, kvc):
    bsz, seq, _ = q.shape
    tq = NSA_Q_TILE
    ncp = seq // NSA_CMP_STRIDE
    n_sel = seq // NSA_SEL_BLOCK
    wlen = NSA_WINDOW + tq
    assert seq % NSA_SEL_K_TILE == 0 and seq >= wlen and n_sel <= LANES and tq == LANES
    swidth = max(wlen, NSA_SEL_K_TILE, ncp)
    sj = np.arange(LANES)[:, None]
    ci = np.arange(ncp)[None, :]
    ovt = np.clip(np.minimum(ci * NSA_CMP_STRIDE + NSA_CMP_BLOCK, sj * NSA_SEL_BLOCK + NSA_SEL_BLOCK)
                  - np.maximum(ci * NSA_CMP_STRIDE, sj * NSA_SEL_BLOCK), 0, None) / NSA_CMP_BLOCK
    ovt = np.where((ci < ncp - 1) & (sj < n_sel), ovt, 0.0)
    rows = NSA_HG * tq
    kvblk = lambda idx: pl.BlockSpec((None, seq, NSA_KV), lambda b, i: (b, 0, idx))
    return pl.pallas_call(
        functools.partial(_nsa_kernel, seq=seq),
        grid=(bsz, seq // tq),
        in_specs=[
            pl.BlockSpec((None, tq, NSA_WIDTH), lambda b, i: (b, i, 0)),
            pl.BlockSpec((None, 2 * NSA_GROUPS, ncp, NSA_HEAD_DIM), lambda b, i: (b, 0, 0, 0)),
            kvblk(2), kvblk(3), kvblk(4), kvblk(5),
            pl.BlockSpec((None, tq, LANES), lambda b, i: (b, i, 0)),
            pl.BlockSpec((LANES, ncp), lambda b, i: (0, 0)),
        ],
        out_specs=pl.BlockSpec((None, tq, NSA_WIDTH), lambda b, i: (b, i, 0)),
        out_shape=jax.ShapeDtypeStruct((bsz, seq, NSA_WIDTH), BF16),
        scratch_shapes=[pltpu.VMEM((NSA_GROUPS, rows, swidth), F32), pltpu.VMEM((NSA_GROUPS, rows, swidth), BF16),
                        pltpu.VMEM((NSA_GROUPS, tq, swidth), F32),
                        pltpu.VMEM((NSA_GROUPS, rows, 1), F32), pltpu.VMEM((NSA_GROUPS, rows, 1), F32),
                        pltpu.VMEM((NSA_GROUPS, rows, NSA_HEAD_DIM), F32),
                        pltpu.VMEM((NSA_GROUPS, rows, NSA_HEAD_DIM), F32)],
        compiler_params=_cp("parallel", "arbitrary"),
        name="nsa_attention",
    )(q, kvc, kv, kv, kv, kv, small, jnp.asarray(ovt, BF16))


def _mla_prep_kernel(c_ref, kr_ref, pos_ref, qnw_ref, kvnw_ref, inv_ref, cq_ref, ckv_ref, k1_ref, k2_ref,
                     cos_ref, sin_ref):
    def rms(v, w):
        return v * lax.rsqrt(jnp.mean(v * v, -1, keepdims=True) + RMS_EPS) * w

    cq_ref[...] = rms(c_ref[:, 0:MLA_Q_RANK], qnw_ref[...]).astype(BF16)
    ckv_ref[...] = rms(c_ref[:, MLA_Q_RANK:MLA_Q_RANK + MLA_KV_RANK], kvnw_ref[...]).astype(BF16)
    ang = pos_ref[...].astype(F32) * inv_ref[...]
    cos = jnp.cos(ang)
    sin = jnp.sin(ang)
    half = MLA_ROPE // 2
    t1 = kr_ref[:, 0:half]
    t2 = kr_ref[:, half:MLA_ROPE]
    k1_ref[...] = (t1 * cos - t2 * sin).astype(BF16)
    k2_ref[...] = (t1 * sin + t2 * cos).astype(BF16)
    cos_ref[...] = cos
    sin_ref[...] = sin


def _mla_prep(c, kr, pos, q_norm_w, kv_norm_w, tm=512):
    m = c.shape[0]
    half = MLA_ROPE // 2
    inv = (1.0 / (ROPE_THETA ** (jnp.arange(0, MLA_ROPE, 2, dtype=F32) / MLA_ROPE)))[None, :]
    row = lambda w: pl.BlockSpec((tm, w), lambda i: (i, 0))
    vec = lambda w: pl.BlockSpec((1, w), lambda i: (0, 0))
    return pl.pallas_call(
        _mla_prep_kernel,
        grid=(m // tm,),
        in_specs=[row(MLA_Q_RANK + MLA_KV_RANK), row(LANES), row(1), vec(MLA_Q_RANK), vec(MLA_KV_RANK), vec(half)],
        out_specs=[row(MLA_Q_RANK), row(MLA_KV_RANK), row(half), row(half), row(half), row(half)],
        out_shape=[jax.ShapeDtypeStruct((m, MLA_Q_RANK), BF16), jax.ShapeDtypeStruct((m, MLA_KV_RANK), BF16),
                   jax.ShapeDtypeStruct((m, half), BF16), jax.ShapeDtypeStruct((m, half), BF16),
                   jax.ShapeDtypeStruct((m, half), F32), jax.ShapeDtypeStruct((m, half), F32)],
        compiler_params=_cp("parallel"),
        name="mla_prep",
    )(c, kr, pos, q_norm_w[None, :], kv_norm_w[None, :], inv)


def _q_rope_kernel(t_ref, cos_ref, sin_ref, e_ref, r1_ref, r2_ref):
    hw = MLA_HEADS * MLA_ROPE // 2
    cos = _dot_f32_by_exact(cos_ref[...], e_ref[...])
    sin = _dot_f32_by_exact(sin_ref[...], e_ref[...])
    t1 = t_ref[:, 0:hw]
    t2 = t_ref[:, hw:2 * hw]
    r1_ref[...] = (t1 * cos - t2 * sin).astype(BF16)
    r2_ref[...] = (t1 * sin + t2 * cos).astype(BF16)


def _q_rope(t, cos, sin, tm=512):
    m = t.shape[0]
    half = MLA_ROPE // 2
    hw = MLA_HEADS * half
    e = (np.arange(hw)[None, :] % half == np.arange(half)[:, None])
    row = lambda w: pl.BlockSpec((tm, w), lambda i: (i, 0))
    return pl.pallas_call(
        _q_rope_kernel,
        grid=(m // tm,),
        in_specs=[row(2 * hw), row(half), row(half), pl.BlockSpec((half, hw), lambda i: (0, 0))],
        out_specs=[row(hw), row(hw)],
        out_shape=[jax.ShapeDtypeStruct((m, hw), BF16)] * 2,
        compiler_params=_cp("parallel"),
        name="mla_q_rope",
    )(t, cos, sin, jnp.asarray(e, BF16))


def _mla_attn_kernel(qn_ref, qr_ref, kn_ref, kr_ref, v_ref, o_ref, s_ref):
    t = MLA_TILE
    seq = qn_ref.shape[0]
    c = (MLA_NOPE + MLA_ROPE) ** -0.5 * float(np.log2(np.e))
    causal = _iota((t, t), 1) <= _iota((t, t), 0)
    for qi in range(seq // t):
        rows = slice(qi * t, (qi + 1) * t)
        qn = qn_ref[rows, :]
        qr = qr_ref[rows, :]
        for kt in range(qi + 1):
            keys = slice(kt * t, (kt + 1) * t)
            s = _dot_nt(qn, kn_ref[keys, :]) + _dot_nt(qr, kr_ref[keys, :])
            s_ref[:, keys] = jnp.where(causal, s, NEG_INF) if kt == qi else s
        n = (qi + 1) * t
        s = s_ref[:, 0:n]
        p = jnp.exp2((s - jnp.max(s, -1, keepdims=True)) * c)
        o = _dot(p, v_ref[0:n, :]) / jnp.sum(p, -1, keepdims=True)
        o_ref[rows, :] = o.astype(o_ref.dtype)


def _mla_attention(qn, qr, kv, kr):
    bsz, seq, _ = qn.shape
    assert seq % MLA_TILE == 0
    return pl.pallas_call(
        _mla_attn_kernel,
        grid=(bsz, MLA_HEADS),
        in_specs=[
            pl.BlockSpec((None, seq, MLA_NOPE), lambda b, h: (b, 0, h)),
            pl.BlockSpec((None, None, seq, MLA_ROPE), lambda b, h: (b, h, 0, 0)),
            pl.BlockSpec((None, seq, MLA_NOPE), lambda b, h: (b, 0, 2 * h)),
            pl.BlockSpec((None, seq, MLA_ROPE), lambda b, h: (b, 0, 0)),
            pl.BlockSpec((None, seq, MLA_V), lambda b, h: (b, 0, 2 * h + 1)),
        ],
        out_specs=pl.BlockSpec((None, seq, MLA_V), lambda b, h: (b, 0, h)),
        out_shape=jax.ShapeDtypeStruct((bsz, seq, MLA_HEADS * MLA_V), BF16),
        scratch_shapes=[pltpu.VMEM((MLA_TILE, seq), F32)],
        compiler_params=_cp("parallel", "arbitrary"),
        name="mla_attention",
    )(qn, qr, kv, kr, kv)


def _hgrn_kernel(hq_ref, hf_ref, hi_ref, hg_ref, lb_ref, nw_ref, o_ref, st):
    C = HGRN_CHUNK

    @pl.when(pl.program_id(2) == 0)
    def _():
        st[...] = jnp.zeros_like(st)

    low = _iota((C, C), 1) <= _iota((C, C), 0)
    tri = jnp.where(low, 1.0, 0.0)
    for c in range(HGRN_TILE // C):
        rs = slice(c * C, (c + 1) * C)
        for j in range(HGRN_HEADS_PER_STEP):
            cs = slice(j * HGRN_KEY, (j + 1) * HGRN_KEY)
            lb = lb_ref[:, cs]
            hf = hf_ref[rs, cs]
            v = hi_ref[rs, cs]
            q = _silu(hq_ref[rs, cs])
            f = lb + (1.0 - lb) * jax.nn.sigmoid(hf)
            k = (1.0 - lb) * jax.nn.sigmoid(-hf)
            b = _dot_exact_by_f32(tri, jnp.log(f))
            b_last = b[C - 1:C, :]
            q_t = q * jnp.exp(b)
            k_t = k * jnp.exp(-b)
            k_end = k * jnp.exp(b_last - b)
            att = jnp.where(low, _dot_nt(q_t, k_t), 0.0)
            s_t = st[j]
            o = _dot(att, v) + _dot_nt(q_t, s_t)
            st[j] = s_t * jnp.exp(b_last) + _dot(v.T, k_end)
            o = o * lax.rsqrt(jnp.mean(o * o, -1, keepdims=True) + RMS_EPS) * nw_ref[...]
            o_ref[rs, cs] = (o * _silu(hg_ref[rs, cs])).astype(o_ref.dtype)


def _hgrn_mixer(hp, lb, norm_w):
    bsz, seq, _ = hp.shape
    t = HGRN_TILE
    hps = HGRN_HEADS_PER_STEP
    w = hps * HGRN_KEY
    nh = HGRN_HEADS // hps
    part = lambda p: pl.BlockSpec((None, t, w), lambda b, h, i: (b, i, p * nh + h))
    return pl.pallas_call(
        _hgrn_kernel,
        grid=(bsz, nh, seq // t),
        in_specs=[part(0), part(1), part(2), part(3),
                  pl.BlockSpec((None, 1, w), lambda b, h, i: (h, 0, 0)),
                  pl.BlockSpec((1, HGRN_VAL), lambda b, h, i: (0, 0))],
        out_specs=pl.BlockSpec((None, t, w), lambda b, h, i: (b, i, h)),
        out_shape=jax.ShapeDtypeStruct((bsz, seq, HGRN_WIDTH), BF16),
        scratch_shapes=[pltpu.VMEM((hps, HGRN_VAL, HGRN_KEY), F32)],
        compiler_params=_cp("parallel", "parallel", "arbitrary"),
        name="hgrn2_mixer",
    )(hp, hp, hp, hp, lb.reshape(nh, 1, w), norm_w[None, :])


def _router_kernel(x_ref, wt_ref, bias_ref, esel_ref, rank_ref, wsel_ref, cnt_ref, carry):
    tm = x_ref.shape[0]
    n_grp = N_EXPERT_GROUPS
    gsz = N_EXPERTS // n_grp
    ninf = -jnp.inf

    @pl.when(pl.program_id(0) == 0)
    def _():
        carry[...] = jnp.zeros_like(carry)

    x = x_ref[...]
    w = wt_ref[...]
    xh = x.astype(BF16)
    xl = (x - xh.astype(F32)).astype(BF16)
    wh = w.astype(BF16)
    wl = (w - wh.astype(F32)).astype(BF16)
    dnt = lambda a, b: lax.dot_general(a, b, (((1,), (1,)), ((), ())), preferred_element_type=F32)
    scores = jax.nn.sigmoid(dnt(wh, xh) + dnt(wh, xl) + dnt(wl, xh))
    biased = scores + bias_ref[...]
    sub = _iota((gsz, tm), 0).astype(F32)

    def top1(blk):
        mx = jnp.max(blk, 0, keepdims=True)
        return mx, jnp.min(jnp.where(blk == mx, sub, 1e9), 0, keepdims=True)

    blocks = [biased[g * gsz:(g + 1) * gsz, :] for g in range(n_grp)]
    grp = []
    for blk in blocks:
        m1, i1 = top1(blk)
        m2, _ = top1(jnp.where(sub == i1, ninf, blk))
        grp.append(m1 + m2)
    grp = jnp.concatenate(grp, axis=0)
    gsel = jnp.zeros((n_grp, tm), F32)
    for _ in range(TOP_GROUPS):
        _, first = top1(grp)
        hit = sub == first
        gsel = jnp.where(hit, 1.0, gsel)
        grp = jnp.where(hit, ninf, grp)

    cand = [jnp.where(gsel[g:g + 1, :] > 0.5, blocks[g], ninf) for g in range(n_grp)]
    eid = [sub + float(g * gsz) for g in range(n_grp)]
    sel = [jnp.zeros((gsz, tm), F32) for _ in range(n_grp)]
    firsts = []
    for _ in range(TOP_K):
        mm = cand[0]
        for g in range(1, n_grp):
            mm = jnp.maximum(mm, cand[g])
        mx = jnp.max(mm, 0, keepdims=True)
        fm = jnp.where(cand[0] == mx, eid[0], 1e9)
        for g in range(1, n_grp):
            fm = jnp.minimum(fm, jnp.where(cand[g] == mx, eid[g], 1e9))
        first = jnp.min(fm, 0, keepdims=True)
        firsts.append(first)
        for g in range(n_grp):
            hit = eid[g] == first
            sel[g] = jnp.where(hit, 1.0, sel[g])
            cand[g] = jnp.where(hit, ninf, cand[g])

    selm = jnp.concatenate(sel, axis=0)
    before = jnp.where(_iota((tm, tm), 0) < _iota((tm, tm), 1), 1.0, 0.0).astype(BF16)
    rank = jnp.dot(selm.astype(BF16), before, preferred_element_type=F32) + carry[...]
    carry[...] = carry[...] + jnp.sum(selm, -1, keepdims=True)
    cnt_ref[...] = carry[...]

    ws, rs = [], []
    for first in firsts:
        wacc = jnp.zeros((gsz, tm), F32)
        racc = jnp.zeros((gsz, tm), F32)
        for g in range(n_grp):
            hit = eid[g] == first
            wacc = jnp.where(hit, scores[g * gsz:(g + 1) * gsz, :], wacc)
            racc = jnp.where(hit, rank[g * gsz:(g + 1) * gsz, :], racc)
        ws.append(jnp.sum(wacc, 0, keepdims=True))
        rs.append(jnp.sum(racc, 0, keepdims=True))
    wsum = ws[0]
    for wk in ws[1:]:
        wsum = wsum + wk
    esel_ref[...] = jnp.concatenate(firsts, axis=0).astype(jnp.int32)
    rank_ref[...] = jnp.concatenate(rs, axis=0).astype(jnp.int32)
    wsel_ref[...] = jnp.concatenate(ws, axis=0) / wsum * ROUTED_SCALE


def _route(x, w_router, router_bias, tm=256):
    T, d = x.shape
    E = N_EXPERTS
    assert N_EXPERT_GROUPS == E // N_EXPERT_GROUPS == 8
    kt = pl.BlockSpec((TOP_K, tm), lambda i: (0, i))
    return pl.pallas_call(
        _router_kernel,
        grid=(T // tm,),
        in_specs=[pl.BlockSpec((tm, d), lambda i: (i, 0)), pl.BlockSpec((E, d), lambda i: (0, 0)),
                  pl.BlockSpec((E, 1), lambda i: (0, 0))],
        out_specs=[kt, kt, kt, pl.BlockSpec((E, 1), lambda i: (0, 0))],
        out_shape=[jax.ShapeDtypeStruct((TOP_K, T), jnp.int32), jax.ShapeDtypeStruct((TOP_K, T), jnp.int32),
                   jax.ShapeDtypeStruct((TOP_K, T), F32), jax.ShapeDtypeStruct((E, 1), F32)],
        scratch_shapes=[pltpu.VMEM((E, 1), F32)],
        compiler_params=_cp("arbitrary"),
        name="moe_router",
    )(x, w_router.T, router_bias.astype(F32)[:, None])


def _expert_kernel(be_ref, nv_ref, x_ref, wg_ref, wu_ref, wd_ref, o_ref, wg_b, wu_b, wd_b):
    i = pl.program_id(0)

    @pl.when((i == 0) | (be_ref[i] != be_ref[jnp.maximum(i - 1, 0)]))
    def _():
        wg_b[...] = wg_ref[...].astype(BF16)
        wu_b[...] = wu_ref[...].astype(BF16)
        wd_b[...] = wd_ref[...].astype(BF16)

    @pl.when(i < nv_ref[0])
    def _():
        half = x_ref.shape[1]
        hi, lo = _unpack_bf16_pairs(x_ref[...])
        xh = hi.astype(BF16)
        xl = lo.astype(BF16)

        def xw(w_b):
            return (jnp.dot(xh, w_b[0:half, :], preferred_element_type=F32)
                    + jnp.dot(xl, w_b[half:, :], preferred_element_type=F32))

        h = _silu(xw(wg_b)) * xw(wu_b)
        o_ref[...] = _pack_bf16_pairs(jnp.dot(h.astype(BF16), wd_b[...], preferred_element_type=F32))

    @pl.when(i >= nv_ref[0])
    def _():
        o_ref[...] = jnp.zeros_like(o_ref)


def _pack_bf16_pairs(y):
    half = y.shape[1] // 2
    bits = lambda v: lax.bitcast_convert_type(v.astype(BF16).astype(F32), jnp.uint32)
    return bits(y[:, :half]) | (bits(y[:, half:]) >> 16)


def _unpack_bf16_pairs(u):
    hi = lax.bitcast_convert_type(u & jnp.uint32(0xFFFF0000), F32)
    lo = lax.bitcast_convert_type(u << 16, F32)
    return hi, lo


def _expert_ffn(x_rows, blk_e, n_valid, w_gate, w_up, w_down, layer, mb):
    p = x_rows.shape[0]
    d = 2 * x_rows.shape[1]
    ff = w_gate.shape[3]
    xmap = lambda i, be, nv: (jnp.minimum(i, nv[0] - 1), 0)
    wmap = lambda i, be, nv: (layer, be[i], 0, 0)
    grid_spec = pltpu.PrefetchScalarGridSpec(
        num_scalar_prefetch=2,
        grid=(p // mb,),
        in_specs=[
            pl.BlockSpec((mb, d // 2), xmap),
            pl.BlockSpec((None, None, d, ff), wmap), pl.BlockSpec((None, None, d, ff), wmap),
            pl.BlockSpec((None, None, ff, d), wmap),
        ],
        out_specs=pl.BlockSpec((mb, d // 2), lambda i, be, nv: (i, 0)),
        scratch_shapes=[pltpu.VMEM((d, ff), BF16), pltpu.VMEM((d, ff), BF16), pltpu.VMEM((ff, d), BF16)],
    )
    return pl.pallas_call(
        _expert_kernel,
        grid_spec=grid_spec,
        out_shape=jax.ShapeDtypeStruct((p, d // 2), jnp.uint32),
        compiler_params=pltpu.CompilerParams(dimension_semantics=("arbitrary",), vmem_limit_bytes=EXPERT_VMEM_LIMIT),
        name="moe_expert_ffn",
    )(blk_e, n_valid, x_rows, w_gate, w_up, w_down)


def _moe_ffn(x, x_pk, layer, w_router, router_bias, w_gate, w_up, w_down, ws_gate, ws_up, ws_down):
    T, d = x.shape
    E = N_EXPERTS
    M = MOE_ROW_BLOCK
    esel, rank, wsel, cnt = _route(x, w_router, router_bias)
    counts = cnt[:, 0].astype(jnp.int32)
    pcounts = (counts + M - 1) // M * M
    pends = jnp.cumsum(pcounts)
    pstarts = pends - pcounts
    NB = -(-(T * TOP_K) // M) + E
    blk_e = jnp.minimum(jnp.sum(pends[None, :] <= (jnp.arange(NB, dtype=jnp.int32) * M)[:, None], axis=1),
                        E - 1).astype(jnp.int32)
    n_valid = (pends[-1] // M).astype(jnp.int32).reshape(1)
    slot = rank + jnp.sum(jnp.where(esel[:, :, None] == jnp.arange(E, dtype=jnp.int32), pstarts, 0), axis=-1)
    slot = slot.reshape(TOP_K * T)
    row_tok = (jnp.arange(NB * M, dtype=jnp.int32) % T).at[slot].set(
        jnp.tile(jnp.arange(T, dtype=jnp.int32), TOP_K), unique_indices=True)
    y_rows = _expert_ffn(x_pk[row_tok], blk_e, n_valid, w_gate, w_up, w_down, layer, M)
    y_k = y_rows[slot].reshape(TOP_K, T, d // 2)

    depth = ws_gate.shape[0]
    ms = SHARED_ROW_BLOCK
    shared = _expert_ffn(x_pk, jnp.zeros((T // ms,), jnp.int32), jnp.full((1,), T // ms, jnp.int32),
                         ws_gate.reshape(depth, 1, d, -1), ws_up.reshape(depth, 1, d, -1),
                         ws_down.reshape(depth, 1, -1, d), layer, ms)
    return shared, y_k, wsel.T


def _mixer_ab(x_bf, bsz, seq, w_in, conv_w, conv_b, dt_bias, a_log, d_skip, norm_w, cmp_pos, cmp_w, w_out):
    T = bsz * seq
    o_zx = SSM_INNER + SSM_CONV_DIM
    o_dt = o_zx + SSM_HEADS
    o_q = o_dt + NSA_WIDTH
    o_kv = o_q + 6 * NSA_KV
    w_bf = w_in.astype(BF16)
    zx = _matmul(x_bf, w_bf[:, :o_zx], F32)
    w_small = _pad_cols(jnp.concatenate([w_bf[:, o_zx:o_dt], w_bf[:, o_kv:]], axis=1), LANES)
    small = _matmul(x_bf, w_small, F32)
    q = _matmul(x_bf, w_bf[:, o_dt:o_q], BF16)
    kv = _matmul(x_bf, w_bf[:, o_q:o_kv], BF16)

    zx3 = zx.reshape(bsz, seq, o_zx)
    small3 = small.reshape(bsz, seq, LANES)
    dt_t = jnp.swapaxes(small3[:, :, :SSM_HEADS], 1, 2)
    y_a = _ssd_mixer(zx3, small3, dt_t, conv_w, conv_b, dt_bias, a_log, d_skip, norm_w)

    kv3 = kv.reshape(bsz, seq, 6 * NSA_KV)
    ncp = seq // NSA_CMP_STRIDE
    strips = kv3[:, :, :2 * NSA_KV].reshape(bsz, ncp, NSA_CMP_STRIDE, 2 * NSA_GROUPS, NSA_HEAD_DIM)
    strips = strips.transpose(0, 3, 1, 2, 4).reshape(bsz, 2 * NSA_GROUPS, ncp, NSA_CMP_STRIDE * NSA_HEAD_DIM)
    kvc = _nsa_compress(strips, cmp_w, cmp_pos)
    y_b = _nsa_mixer(q.reshape(bsz, seq, NSA_WIDTH), kv3, small3, kvc)

    return _matmul_cat(y_a.reshape(T, SSM_INNER), y_b.reshape(T, NSA_WIDTH), w_out.astype(BF16), F32)


def _mixer_cd(x_bf, bsz, seq, positions, w_in, q_norm_w, w_uq, kv_norm_w, w_ukv, lb, hgrn_norm_w, w_out):
    T = bsz * seq
    o_c = MLA_Q_RANK + MLA_KV_RANK
    o_kr = o_c + MLA_ROPE
    half = MLA_ROPE // 2
    w_bf = w_in.astype(BF16)
    c = _matmul(x_bf, w_bf[:, :o_c], F32)
    kr = _matmul(x_bf, _pad_cols(w_bf[:, o_c:o_kr], LANES), F32)
    hp = _matmul(x_bf, w_bf[:, o_kr:], F32)

    cqn, ckvn, k1, k2, cos, sin = _mla_prep(c, kr, positions.reshape(T, 1), q_norm_w, kv_norm_w)
    wq = w_uq.astype(BF16).reshape(MLA_Q_RANK, MLA_HEADS, MLA_NOPE + MLA_ROPE)
    w_nope = wq[:, :, :MLA_NOPE].reshape(MLA_Q_RANK, MLA_HEADS * MLA_NOPE)
    w_rope = jnp.concatenate([wq[:, :, MLA_NOPE:MLA_NOPE + half].reshape(MLA_Q_RANK, MLA_HEADS * half),
                              wq[:, :, MLA_NOPE + half:].reshape(MLA_Q_RANK, MLA_HEADS * half)], axis=1)
    qn = _matmul(cqn, w_nope, BF16)
    r1, r2 = _q_rope(_matmul(cqn, w_rope, F32), cos, sin)
    qr = jnp.concatenate([r1.reshape(bsz, seq, MLA_HEADS, half), r2.reshape(bsz, seq, MLA_HEADS, half)], axis=-1)
    qr = qr.transpose(0, 2, 1, 3)
    kvu = _matmul(ckvn, w_ukv.astype(BF16), BF16)
    y_c = _mla_attention(qn.reshape(bsz, seq, -1), qr, kvu.reshape(bsz, seq, -1),
                         jnp.concatenate([k1, k2], axis=1).reshape(bsz, seq, MLA_ROPE))
    y_d = _hgrn_mixer(hp.reshape(bsz, seq, -1), lb, hgrn_norm_w)
    return _matmul_cat(y_c.reshape(T, -1), y_d.reshape(T, -1), w_out.astype(BF16), F32)


def kernel(x, positions, ab_w_in, ssm_conv_w, ssm_conv_b, ssm_dt_bias, ssm_a_log, ssm_d, ssm_norm_w, nsa_cmp_pos, nsa_cmp_w, ab_w_out, cd_w_in, mla_q_norm_w, mla_w_uq, mla_kv_norm_w, mla_w_ukv, hgrn_lb_logits, hgrn_norm_w, cd_w_out, ln_g, ln_b, moe_w_router, moe_router_bias, moe_w_gate, moe_w_up, moe_w_down, moe_shared_w_gate, moe_shared_w_up, moe_shared_w_down):
    bsz, seq, d = x.shape
    T = bsz * seq
    lb_all = jnp.cumsum(jax.nn.softmax(hgrn_lb_logits.astype(F32), axis=0), axis=0)
    lb_all = lb_all - lb_all[0]
    xf = x.reshape(T, d)
    x_bf = xf.astype(BF16)
    for l in range(DEPTH):
        i = l // 2
        if l % 2 == 0:
            mix = _mixer_ab(x_bf, bsz, seq, ab_w_in[i], ssm_conv_w[i], ssm_conv_b[i], ssm_dt_bias[i], ssm_a_log[i],
                            ssm_d[i], ssm_norm_w[i], nsa_cmp_pos[i], nsa_cmp_w[i], ab_w_out[i])
        else:
            mix = _mixer_cd(x_bf, bsz, seq, positions, cd_w_in[i], mla_q_norm_w[i], mla_w_uq[i], mla_kv_norm_w[i],
                            mla_w_ukv[i], lb_all[l], hgrn_norm_w[i], cd_w_out[i])
        xf, x_pk = _deepnorm_ln(xf, mix, ln_g[l, 0], ln_b[l, 0])
        shared, y_k, w_k = _moe_ffn(xf, x_pk, l, moe_w_router[l], moe_router_bias[l], moe_w_gate, moe_w_up,
                                    moe_w_down, moe_shared_w_gate, moe_shared_w_up, moe_shared_w_down)
        xf, x_bf = _deepnorm_ln_combine(xf, shared, y_k, w_k, ln_g[l, 1], ln_b[l, 1])
    return xf.reshape(bsz, seq, d)
```
